```python
import math
import jax, jax.numpy as jnp
from jax import lax
import numpy as np

D_MODEL = 1024
BATCH = 8
SEQ = 2048
DEPTH = 2

D_MIX = D_MODEL
ATTN_HEADS = 8
HEAD_DIM = 64
ATTN_WIDTH = ATTN_HEADS * HEAD_DIM
SSD_HEADS = 8
SSD_HEAD_DIM = 64
SSD_WIDTH = SSD_HEADS * SSD_HEAD_DIM
SSD_GROUPS = 2
SSD_STATE = 128
CONV_K = 4
CHUNK = 128
Q_BLOCK = 128
CONV_DIM = SSD_WIDTH + 2 * SSD_GROUPS * SSD_STATE
D_IN_PROJ = 3 * ATTN_WIDTH + SSD_WIDTH + CONV_DIM + SSD_HEADS
D_FF = 2816
N_EXPERTS = 8
TOP_K = 2
D_FF_EXPERT = 3584
N_DENSE = (DEPTH + 1) // 2
N_MOE = DEPTH // 2
EPS = 1e-5

kernel_name = "hymba_sbattn_ssd_moe_block"


def rmsnorm(x, g):
    xf = x.astype(jnp.float32)
    var = jnp.mean(xf * xf, axis=-1, keepdims=True)
    return (xf * lax.rsqrt(var + EPS) * g.astype(jnp.float32)).astype(x.dtype)


def stick_breaking_attention(q, k, v):
    seq = q.shape[1]
    scale = HEAD_DIM ** -0.5
    outs = []
    for blk in range(seq // Q_BLOCK):
        q0 = blk * Q_BLOCK
        q1 = q0 + Q_BLOCK
        qb = q[:, q0:q1]
        kb = k[:, :q1]
        vb = v[:, :q1]
        z = jnp.einsum("bthd,bshd->bhts", qb, kb).astype(jnp.float32) * scale
        t_pos = q0 + jnp.arange(Q_BLOCK)[:, None]
        s_pos = jnp.arange(q1)[None, :]
        causal = s_pos < t_pos
        neg_log_1m_beta = jnp.where(causal, jax.nn.softplus(z), 0.0)
        later = lax.cumsum(neg_log_1m_beta, axis=3, reverse=True) - neg_log_1m_beta
        w = jnp.where(causal, jnp.exp(jax.nn.log_sigmoid(z) - later), 0.0)
        outs.append(jnp.einsum("bhts,bshd->bthd", w.astype(v.dtype), vb))
    return jnp.concatenate(outs, axis=1)


def segsum(a):
    t = a.shape[-1]
    ar = jnp.broadcast_to(a[..., :, None], a.shape + (t,))
    strict = jnp.tril(jnp.ones((t, t), dtype=bool), -1)
    cs = jnp.cumsum(jnp.where(strict, ar, 0.0), axis=-2)
    incl = jnp.tril(jnp.ones((t, t), dtype=bool), 0)
    return jnp.where(incl, cs, -jnp.inf)


def ssd_chunked(x, dt, a, b_in, c_in):
    bsz, seq = x.shape[0], x.shape[1]
    nc = seq // CHUNK
    r = SSD_HEADS // SSD_GROUPS
    xc = (x * dt[..., None]).reshape(bsz, nc, CHUNK, SSD_GROUPS, r, SSD_HEAD_DIM)
    ac = (dt * a).reshape(bsz, nc, CHUNK, SSD_GROUPS, r).transpose(0, 3, 4, 1, 2)
    bc = b_in.reshape(bsz, nc, CHUNK, SSD_GROUPS, SSD_STATE)
    cc = c_in.reshape(bsz, nc, CHUNK, SSD_GROUPS, SSD_STATE)
    a_cum = jnp.cumsum(ac, axis=-1)
    decay = jnp.exp(segsum(ac))
    cb = jnp.einsum("bclgn,bcsgn->bgcls", cc, bc)
    y_diag = jnp.einsum("bgcls,bgrcls,bcsgrp->bclgrp", cb, decay, xc)
    decay_states = jnp.exp(a_cum[..., -1:] - a_cum)
    states = jnp.einsum("bclgn,bgrcl,bclgrp->bcgrpn", bc, decay_states, xc)
    states = jnp.concatenate([jnp.zeros_like(states[:, :1]), states], axis=1)
    last = jnp.pad(a_cum[..., -1], ((0, 0), (0, 0), (0, 0), (1, 0)))
    chunk_decay = jnp.exp(segsum(last))
    new_states = jnp.einsum("bgrzc,bcgrpn->bzgrpn", chunk_decay, states)
    states = new_states[:, :-1]
    y_off = jnp.einsum("bclgn,bcgrpn,bgrcl->bclgrp", cc, states, jnp.exp(a_cum))
    return (y_diag + y_off).reshape(bsz, seq, SSD_HEADS, SSD_HEAD_DIM)


def causal_depthwise_conv(u, w, b):
    out = lax.conv_general_dilated(
        u, w[:, None, :].astype(u.dtype), window_strides=(1,), padding=[(CONV_K - 1, 0)],
        dimension_numbers=("NWC", "WIO", "NWC"), feature_group_count=CONV_DIM)
    return out + b.astype(u.dtype)


def hybrid_mixer(h, w_in, conv_w, conv_b, dt_bias, a_log, d_skip, attn_norm_g, ssd_norm_g, w_out):
    bsz, seq = h.shape[0], h.shape[1]
    proj = h @ w_in
    o1 = ATTN_WIDTH
    o2 = o1 + ATTN_WIDTH
    o3 = o2 + ATTN_WIDTH
    o4 = o3 + SSD_WIDTH
    o5 = o4 + CONV_DIM
    q, k, v, z, xbc, dt_raw = jnp.split(proj, [o1, o2, o3, o4, o5], axis=-1)
    hs = (bsz, seq, ATTN_HEADS, HEAD_DIM)
    attn = stick_breaking_attention(q.reshape(hs), k.reshape(hs), v.reshape(hs))
    attn = rmsnorm(attn.reshape(bsz, seq, ATTN_WIDTH), attn_norm_g)
    xbc = jax.nn.silu(causal_depthwise_conv(xbc, conv_w, conv_b))
    xs, bs, cs = jnp.split(xbc, [SSD_WIDTH, SSD_WIDTH + SSD_GROUPS * SSD_STATE], axis=-1)
    xs = xs.astype(jnp.float32).reshape(bsz, seq, SSD_HEADS, SSD_HEAD_DIM)
    bs = bs.astype(jnp.float32).reshape(bsz, seq, SSD_GROUPS, SSD_STATE)
    cs = cs.astype(jnp.float32).reshape(bsz, seq, SSD_GROUPS, SSD_STATE)
    dt = jax.nn.softplus(dt_raw.astype(jnp.float32) + dt_bias.astype(jnp.float32))
    a = -jnp.exp(a_log.astype(jnp.float32))
    y = ssd_chunked(xs, dt, a, bs, cs) + d_skip.astype(jnp.float32)[:, None] * xs
    y = y.reshape(bsz, seq, SSD_WIDTH) * jax.nn.silu(z.astype(jnp.float32))
    yg = y.reshape(bsz, seq, SSD_GROUPS, SSD_WIDTH // SSD_GROUPS)
    yg = yg * lax.rsqrt(jnp.mean(yg * yg, axis=-1, keepdims=True) + EPS)
    y = (yg.reshape(bsz, seq, SSD_WIDTH) * ssd_norm_g.astype(jnp.float32)).astype(h.dtype)
    return jnp.concatenate([attn, y], axis=-1) @ w_out


def swiglu(h, wg, wu, wd):
    return (jax.nn.silu(h @ wg) * (h @ wu)) @ wd


def moe_swiglu(h, router_w, wg, wu, wd):
    logits = (h @ router_w).astype(jnp.float32)
    top_vals, top_idx = lax.top_k(logits, TOP_K)
    top_w = jax.nn.softmax(top_vals, axis=-1)
    gates = jnp.sum(jax.nn.one_hot(top_idx, N_EXPERTS, dtype=jnp.float32) * top_w[..., None], axis=-2)
    gates = gates.astype(h.dtype)
    out = jnp.zeros_like(h)
    for e in range(N_EXPERTS):
        out = out + gates[..., e:e + 1] * swiglu(h, wg[e], wu[e], wd[e])
    return out


def setup_inputs(seed: int = 0) -> dict:
    key = jax.random.key(seed)
    ks = jax.random.split(key, 24)
    f32 = jnp.float32

    def nrm(k, shape, fan_in):
        return jax.random.normal(k, shape, f32) * (fan_in ** -0.5)

    def gain(k, shape):
        return 1.0 + 0.02 * jax.random.normal(k, shape, f32)

    dt0 = jnp.exp(jax.random.uniform(ks[5], (DEPTH, SSD_HEADS), f32, math.log(1e-3), math.log(1e-1)))
    dt_bias = dt0 + jnp.log(-jnp.expm1(-dt0))
    return {
        "x": jax.random.normal(ks[0], (BATCH, SEQ, D_MODEL), f32),
        "mix_norm_g": gain(ks[1], (DEPTH, D_MODEL)),
        "w_in": nrm(ks[2], (DEPTH, D_MODEL, D_IN_PROJ), D_MODEL),
        "conv_w": nrm(ks[3], (DEPTH, CONV_K, CONV_DIM), CONV_K),
        "conv_b": 0.01 * jax.random.normal(ks[4], (DEPTH, CONV_DIM), f32),
        "dt_bias": dt_bias,
        "a_log": jnp.log(jax.random.uniform(ks[6], (DEPTH, SSD_HEADS), f32, 1.0, 16.0)),
        "d_skip": gain(ks[7], (DEPTH, SSD_HEADS)),
        "attn_norm_g": gain(ks[8], (DEPTH, ATTN_WIDTH)),
        "ssd_norm_g": gain(ks[9], (DEPTH, SSD_WIDTH)),
        "w_out": nrm(ks[10], (DEPTH, D_MIX, D_MODEL), D_MIX),
        "ffn_norm_g": gain(ks[11], (DEPTH, D_MODEL)),
        "dense_w_gate": nrm(ks[12], (N_DENSE, D_MODEL, D_FF), D_MODEL),
        "dense_w_up": nrm(ks[13], (N_DENSE, D_MODEL, D_FF), D_MODEL),
        "dense_w_down": nrm(ks[14], (N_DENSE, D_FF, D_MODEL), D_FF),
        "router_w": nrm(ks[15], (N_MOE, D_MODEL, N_EXPERTS), D_MODEL),
        "moe_w_gate": nrm(ks[16], (N_MOE, N_EXPERTS, D_MODEL, D_FF_EXPERT), D_MODEL),
        "moe_w_up": nrm(ks[17], (N_MOE, N_EXPERTS, D_MODEL, D_FF_EXPERT), D_MODEL),
        "moe_w_down": nrm(ks[18], (N_MOE, N_EXPERTS, D_FF_EXPERT, D_MODEL), D_FF_EXPERT),
        "final_norm_g": gain(ks[19], (D_MODEL,)),
    }


def reference(x, mix_norm_g, w_in, conv_w, conv_b, dt_bias, a_log, d_skip, attn_norm_g, ssd_norm_g,
              w_out, ffn_norm_g, dense_w_gate, dense_w_up, dense_w_down, router_w, moe_w_gate,
              moe_w_up, moe_w_down, final_norm_g):
    for layer in range(DEPTH):
        h = rmsnorm(x, mix_norm_g[layer])
        x = x + hybrid_mixer(h, w_in[layer], conv_w[layer], conv_b[layer], dt_bias[layer], a_log[layer],
                             d_skip[layer], attn_norm_g[layer], ssd_norm_g[layer], w_out[layer])
        h = rmsnorm(x, ffn_norm_g[layer])
        i = layer // 2
        if layer % 2 == 0:
            x = x + swiglu(h, dense_w_gate[i], dense_w_up[i], dense_w_down[i])
        else:
            x = x + moe_swiglu(h, router_w[i], moe_w_gate[i], moe_w_up[i], moe_w_down[i])
    return rmsnorm(x, final_norm_g)
```

```python
import functools

import jax
import jax.numpy as jnp
from jax import lax
from jax.experimental import pallas as pl
from jax.experimental.pallas import tpu as pltpu

F32 = jnp.float32
BF16 = jnp.bfloat16
HIGHEST = lax.Precision.HIGHEST

HEAD_DIM = 64
HEADS_PER_VREG = 2
LANES = 128
SSD_GROUPS = 2
SSD_STATE = 128
CONV_K = 4
CONV_HALO = 8
TOP_K = 2
EPS = 1e-5
NEG_BIG = -1e30

ROW_TILE = 512
ATTN_TILE = 256
SSD_CHUNK = 256
TOKEN_BLOCK = 512
EXPERT_TILE = 512
VMEM_LIMIT = 56 * 1024 * 1024


def _params(semantics):
    return pltpu.CompilerParams(dimension_semantics=semantics, vmem_limit_bytes=VMEM_LIMIT)


def _rms(x, g):
    var = jnp.mean(x * x, axis=-1, keepdims=True)
    return x * lax.rsqrt(var + EPS) * g


def _silu(x):
    return x / (1.0 + jnp.exp(-x))


def _softplus(x):
    return jnp.maximum(x, 0.0) + jnp.log1p(jnp.exp(-jnp.abs(x)))


def _dot(a, b):
    return jnp.dot(a, b, preferred_element_type=F32)


def _dot_nt(a, b, precision=None):
    return lax.dot_general(a, b, (((1,), (1,)), ((), ())), precision=precision,
                           preferred_element_type=F32)


def _inproj_kernel(x_ref, g_ref, w_ref, wdt_ref, wdtt_ref, proj_ref, dtc_ref, dtr_ref):
    h = _rms(x_ref[...], g_ref[...])
    proj_ref[...] = _dot(h.astype(BF16), w_ref[...]).astype(proj_ref.dtype)
    n_dt = dtc_ref.shape[-1]
    dtc_ref[...] = jnp.dot(h, wdt_ref[...], precision=HIGHEST, preferred_element_type=F32)[:, :n_dt]
    dtr_ref[...] = _dot_nt(wdtt_ref[...], h, precision=HIGHEST)


def _inproj(x2, g, w_main, w_dt):
    t, d = x2.shape
    n = w_main.shape[1]
    nh = w_dt.shape[1]
    w_dt_pad = jnp.pad(w_dt, ((0, 0), (0, LANES - nh)))
    tm = min(ROW_TILE, t)
    return pl.pallas_call(
        _inproj_kernel,
        grid=(t // tm,),
        in_specs=[
            pl.BlockSpec((tm, d), lambda i: (i, 0)),
            pl.BlockSpec((1, d), lambda i: (0, 0)),
            pl.BlockSpec((d, n), lambda i: (0, 0)),
            pl.BlockSpec((d, LANES), lambda i: (0, 0)),
            pl.BlockSpec((nh, d), lambda i: (0, 0)),
        ],
        out_specs=[
            pl.BlockSpec((tm, n), lambda i: (i, 0)),
            pl.BlockSpec((tm, nh), lambda i: (i, 0)),
            pl.BlockSpec((nh, tm), lambda i: (0, i)),
        ],
        out_shape=[
            jax.ShapeDtypeStruct((t, n), BF16),
            jax.ShapeDtypeStruct((t, nh), F32),
            jax.ShapeDtypeStruct((nh, t), F32),
        ],
        compiler_params=_params(("parallel",)),
        name="inproj",
    )(x2, g.reshape(1, d), w_main.astype(BF16), w_dt_pad, w_dt.T)


def _attn_kernel(q_ref, k_ref, v_ref, o_ref, *, tile):
    i = pl.program_id(2)
    lane = lax.broadcasted_iota(jnp.int32, (tile, LANES), 1)
    row = lax.broadcasted_iota(jnp.int32, (tile, tile), 0)
    col = lax.broadcasted_iota(jnp.int32, (tile, tile), 1)
    m_incl = jnp.where(row >= col, 1.0, 0.0).astype(BF16)
    causal = col < row
    q_all = q_ref[...] * (HEAD_DIM ** -0.5)

    def block(qh, j, carry, acc, masked):
        start = pl.multiple_of(j * tile, tile)
        ks = k_ref[pl.ds(start, tile), :]
        vs = v_ref[pl.ds(start, tile), :]
        z = _dot_nt(qh, ks)
        sp = _softplus(z)
        if masked:
            sp = jnp.where(causal, sp, 0.0)
        cs = _dot(sp.astype(BF16), m_incl) + carry
        w = jnp.exp(z - cs)
        if masked:
            w = jnp.where(causal, w, 0.0)
        acc = acc + _dot(w.astype(BF16), vs)
        return cs[:, 0:1], acc

    outs = []
    for hh in range(HEADS_PER_VREG):
        head_lanes = (lane >= hh * HEAD_DIM) & (lane < (hh + 1) * HEAD_DIM)
        qh = jnp.where(head_lanes, q_all, 0.0).astype(BF16)
        carry0 = jnp.zeros((tile, 1), F32)
        acc0 = jnp.zeros((tile, LANES), F32)
        carry, acc = block(qh, i, carry0, acc0, True)

        def body(jj, state, qh=qh):
            c, a = state
            return block(qh, i - jj, c, a, False)

        carry, acc = lax.fori_loop(1, i + 1, body, (carry, acc))
        outs.append(jnp.where(head_lanes, acc, 0.0))
    o_ref[...] = (outs[0] + outs[1]).astype(o_ref.dtype)


def _attention(proj3, attn_width):
    b, s, _ = proj3.shape
    tile = min(ATTN_TILE, s)
    n_pairs = attn_width // LANES
    return pl.pallas_call(
        functools.partial(_attn_kernel, tile=tile),
        grid=(b, n_pairs, s // tile),
        in_specs=[
            pl.BlockSpec((None, tile, LANES), lambda bi, p, i: (bi, i, p)),
            pl.BlockSpec((None, s, LANES), lambda bi, p, i: (bi, 0, n_pairs + p)),
            pl.BlockSpec((None, s, LANES), lambda bi, p, i: (bi, 0, 2 * n_pairs + p)),
        ],
        out_specs=pl.BlockSpec((None, tile, LANES), lambda bi, p, i: (bi, i, p)),
        out_shape=jax.ShapeDtypeStruct((b, s, attn_width), BF16),
        compiler_params=_params(("parallel", "parallel", "arbitrary")),
        name="sb_attention",
    )(proj3, proj3, proj3)


def _ssd_kernel(xbc_ref, z_ref, dtc_ref, dtr_ref, cw_ref, cb_ref, dtb_c_ref, dtb_r_ref, alog_c_ref,
                alog_r_ref, dskip_ref, gn_ref, expand_ref, y_ref, ext_ref, state_ref, *, chunk, width, heads):
    c = pl.program_id(1)
    heads_per_group = heads // SSD_GROUPS
    group_width = heads_per_group * HEAD_DIM
    gs = SSD_GROUPS * SSD_STATE

    @pl.when(c == 0)
    def _():
        ext_ref[0:CONV_HALO, :] = jnp.zeros((CONV_HALO, ext_ref.shape[1]), F32)
        state_ref[...] = jnp.zeros(state_ref.shape, F32)

    ext_ref[CONV_HALO:CONV_HALO + chunk, :] = xbc_ref[...].astype(F32)
    conv = cb_ref[...]
    for k in range(CONV_K):
        conv = conv + cw_ref[k:k + 1, :] * ext_ref[pl.ds(CONV_HALO - (CONV_K - 1) + k, chunk), :]
    ext_ref[0:CONV_HALO, :] = ext_ref[chunk:chunk + CONV_HALO, :]
    xbc = _silu(conv)
    xs = xbc[:, :width]
    bm = xbc[:, width:width + gs]
    cm = xbc[:, width + gs:]

    dt_c = _softplus(dtc_ref[...] + dtb_c_ref[...])
    dt_r = _softplus(dtr_ref[...] + dtb_r_ref[...])
    da_c = dt_c * (-jnp.exp(alog_c_ref[...]))
    da_r = dt_r * (-jnp.exp(alog_r_ref[...]))
    row = lax.broadcasted_iota(jnp.int32, (chunk, chunk), 0)
    col = lax.broadcasted_iota(jnp.int32, (chunk, chunk), 1)
    lower = row >= col
    tri_l = jnp.where(lower, 1.0, 0.0)
    tri_u = jnp.where(row <= col, 1.0, 0.0)
    cum_c = jnp.dot(tri_l, da_c, precision=HIGHEST, preferred_element_type=F32)
    cum_r = jnp.dot(da_r, tri_u, precision=HIGHEST, preferred_element_type=F32)
    last_c = cum_c[chunk - 1:chunk, :]
    last_r = cum_r[:, chunk - 1:chunk]
    tail_r = jnp.exp(last_r - cum_r)

    expand = expand_ref[...]
    dt_x = jnp.dot(dt_c, expand, precision=HIGHEST, preferred_element_type=F32)
    head_x = jnp.dot(jnp.exp(cum_c), expand, precision=HIGHEST, preferred_element_type=F32)
    last_x = jnp.dot(jnp.exp(last_c), expand, precision=HIGHEST, preferred_element_type=F32)
    xdt = xs * dt_x
    lane = lax.broadcasted_iota(jnp.int32, (chunk, LANES), 1)

    y_parts = []
    new_state = []
    for g in range(SSD_GROUPS):
        b_g = bm[:, g * SSD_STATE:(g + 1) * SSD_STATE]
        c_g = cm[:, g * SSD_STATE:(g + 1) * SSD_STATE]
        cb = _dot_nt(c_g.astype(BF16), b_g.astype(BF16))
        bt_g = b_g.T
        st_g = state_ref[:, g * group_width:(g + 1) * group_width]
        y_off = _dot(c_g.astype(BF16), st_g.astype(BF16))
        for p in range(group_width // LANES):
            lo = g * group_width + p * LANES
            xdt_p = xdt[:, lo:lo + LANES]
            y_p = jnp.zeros((chunk, LANES), F32)
            s_p = jnp.zeros((SSD_STATE, LANES), F32)
            for hh in range(HEADS_PER_VREG):
                h = lo // HEAD_DIM + hh
                head_lanes = (lane >= hh * HEAD_DIM) & (lane < (hh + 1) * HEAD_DIM)
                x_h = jnp.where(head_lanes, xdt_p, 0.0).astype(BF16)
                seg = cum_c[:, h:h + 1] - cum_r[h:h + 1, :]
                decay = jnp.exp(jnp.where(lower, seg, NEG_BIG))
                y_p = y_p + _dot((cb * decay).astype(BF16), x_h)
                s_p = s_p + _dot((bt_g * tail_r[h:h + 1, :]).astype(BF16), x_h)
            y_parts.append(y_p + y_off[:, p * LANES:(p + 1) * LANES] * head_x[:, lo:lo + LANES])
            new_state.append(s_p)
    y = jnp.concatenate(y_parts, axis=1) + dskip_ref[...] * xs
    state_ref[...] = state_ref[...] * last_x + jnp.concatenate(new_state, axis=1)

    y = y * _silu(z_ref[...].astype(F32))
    normed = []
    for g in range(SSD_GROUPS):
        yg = y[:, g * group_width:(g + 1) * group_width]
        normed.append(yg * lax.rsqrt(jnp.mean(yg * yg, axis=-1, keepdims=True) + EPS))
    y_ref[...] = (jnp.concatenate(normed, axis=1) * gn_ref[...]).astype(y_ref.dtype)


def _ssd(proj3, dtc3, dtr, conv_w, conv_b, dt_bias, a_log, d_skip, ssd_norm_g, attn_width):
    b, s, _ = proj3.shape
    heads = dt_bias.shape[0]
    width = heads * HEAD_DIM
    conv_dim = conv_w.shape[1]
    chunk = min(SSD_CHUNK, s)
    nc = s // chunk
    z_blk = 3 * attn_width // width
    xbc_blk = (3 * attn_width + width) // conv_dim
    expand = jnp.repeat(jnp.eye(heads, dtype=F32), HEAD_DIM, axis=1)
    small = lambda shape: pl.BlockSpec(shape, lambda bi, ci: (0, 0))
    return pl.pallas_call(
        functools.partial(_ssd_kernel, chunk=chunk, width=width, heads=heads),
        grid=(b, nc),
        in_specs=[
            pl.BlockSpec((None, chunk, conv_dim), lambda bi, ci: (bi, ci, xbc_blk)),
            pl.BlockSpec((None, chunk, width), lambda bi, ci: (bi, ci, z_blk)),
            pl.BlockSpec((None, chunk, heads), lambda bi, ci: (bi, ci, 0)),
            pl.BlockSpec((heads, chunk), lambda bi, ci: (0, bi * nc + ci)),
            small((CONV_K, conv_dim)), small((1, conv_dim)),
            small((1, heads)), small((heads, 1)), small((1, heads)), small((heads, 1)),
            small((1, width)), small((1, width)), small((heads, width)),
        ],
        out_specs=pl.BlockSpec((None, chunk, width), lambda bi, ci: (bi, ci, 0)),
        out_shape=jax.ShapeDtypeStruct((b, s, width), BF16),
        scratch_shapes=[
            pltpu.VMEM((chunk + CONV_HALO, conv_dim), F32),
            pltpu.VMEM((SSD_STATE, width), F32),
        ],
        compiler_params=_params(("parallel", "arbitrary")),
        name="conv_ssd",
    )(proj3, proj3, dtc3, dtr, conv_w, conv_b.reshape(1, conv_dim),
      dt_bias.reshape(1, heads), dt_bias.reshape(heads, 1), a_log.reshape(1, heads), a_log.reshape(heads, 1),
      jnp.repeat(d_skip, HEAD_DIM).reshape(1, width), ssd_norm_g.reshape(1, width), expand)


def _outproj_kernel(attn_ref, y_ref, x_ref, ga_ref, wa_ref, wy_ref, gf_ref, x1_ref, h_ref):
    a = _rms(attn_ref[...].astype(F32), ga_ref[...]).astype(BF16)
    x1 = x_ref[...] + _dot(a, wa_ref[...]) + _dot(y_ref[...], wy_ref[...])
    x1_ref[...] = x1
    h_ref[...] = _rms(x1, gf_ref[...]).astype(h_ref.dtype)


def _outproj(attn2, y2, x2, attn_norm_g, w_out, ffn_norm_g):
    t, d = x2.shape
    wa = attn2.shape[1]
    wy = y2.shape[1]
    tm = min(ROW_TILE, t)
    w_bf = w_out.astype(BF16)
    return pl.pallas_call(
        _outproj_kernel,
        grid=(t // tm,),
        in_specs=[
            pl.BlockSpec((tm, wa), lambda i: (i, 0)),
            pl.BlockSpec((tm, wy), lambda i: (i, 0)),
            pl.BlockSpec((tm, d), lambda i: (i, 0)),
            pl.BlockSpec((1, wa), lambda i: (0, 0)),
            pl.BlockSpec((wa, d), lambda i: (0, 0)),
            pl.BlockSpec((wy, d), lambda i: (0, 0)),
            pl.BlockSpec((1, d), lambda i: (0, 0)),
        ],
        out_specs=[pl.BlockSpec((tm, d), lambda i: (i, 0)), pl.BlockSpec((tm, d), lambda i: (i, 0))],
        out_shape=[jax.ShapeDtypeStruct((t, d), F32), jax.ShapeDtypeStruct((t, d), BF16)],
        compiler_params=_params(("parallel",)),
        name="outproj",
    )(attn2, y2, x2, attn_norm_g.reshape(1, wa), w_bf[:wa], w_bf[wa:], ffn_norm_g.reshape(1, d))


def _dense_ffn_kernel(h_ref, x_ref, wg_ref, wu_ref, wd_ref, gfin_ref, o_ref, acc_ref, *, final_norm):
    j = pl.program_id(1)
    h = h_ref[...]
    act = (_silu(_dot(h, wg_ref[...])) * _dot(h, wu_ref[...])).astype(BF16)
    part = _dot(act, wd_ref[...])

    @pl.when(j == 0)
    def _():
        acc_ref[...] = x_ref[...] + part

    @pl.when(j > 0)
    def _():
        acc_ref[...] += part

    @pl.when(j == pl.num_programs(1) - 1)
    def _():
        out = acc_ref[...]
        o_ref[...] = _rms(out, gfin_ref[...]) if final_norm else out


def _ff_chunk(d_ff, target):
    best = LANES
    for c in range(LANES, min(d_ff, target) + 1, LANES):
        if d_ff % c == 0:
            best = c
    return best


def _dense_ffn(h2, x1, wg, wu, wd, final_g, final_norm):
    t, d = x1.shape
    d_ff = wg.shape[1]
    tm = min(ROW_TILE, t)
    fc = _ff_chunk(d_ff, 1408)
    return pl.pallas_call(
        functools.partial(_dense_ffn_kernel, final_norm=final_norm),
        grid=(t // tm, d_ff // fc),
        in_specs=[
            pl.BlockSpec((tm, d), lambda i, j: (i, 0)),
            pl.BlockSpec((tm, d), lambda i, j: (i, 0)),
            pl.BlockSpec((d, fc), lambda i, j: (0, j)),
            pl.BlockSpec((d, fc), lambda i, j: (0, j)),
            pl.BlockSpec((fc, d), lambda i, j: (j, 0)),
            pl.BlockSpec((1, d), lambda i, j: (0, 0)),
        ],
        out_specs=pl.BlockSpec((tm, d), lambda i, j: (i, 0)),
        out_shape=jax.ShapeDtypeStruct((t, d), F32),
        scratch_shapes=[pltpu.VMEM((tm, d), F32)],
        compiler_params=_params(("parallel", "arbitrary")),
        name="dense_ffn",
    )(h2, x1, wg.astype(BF16), wu.astype(BF16), wd.astype(BF16), final_g.reshape(1, d))


def _router_kernel(x_ref, g_ref, rwt_ref, ri_ref, rg_ref, cum_ref, carry_ref, *, n_exp):
    blk = x_ref.shape[0]

    @pl.when(pl.program_id(0) == 0)
    def _():
        carry_ref[...] = jnp.zeros(carry_ref.shape, F32)

    h = _rms(x_ref[...], g_ref[...])
    logits = _dot_nt(rwt_ref[...], h, precision=HIGHEST)
    eidx = lax.broadcasted_iota(jnp.int32, (n_exp, blk), 0)
    m1 = jnp.max(logits, axis=0, keepdims=True)
    i1 = jnp.min(jnp.where(logits == m1, eidx, n_exp), axis=0, keepdims=True)
    rest = jnp.where(eidx == i1, -jnp.inf, logits)
    m2 = jnp.max(rest, axis=0, keepdims=True)
    i2 = jnp.min(jnp.where(rest == m2, eidx, n_exp), axis=0, keepdims=True)
    e21 = jnp.exp(m2 - m1)
    g1 = 1.0 / (1.0 + e21)
    g2 = e21 * g1

    sel1 = eidx == i1
    sel2 = eidx == i2
    onehot = jnp.where(sel1 | sel2, 1.0, 0.0)
    row = lax.broadcasted_iota(jnp.int32, (blk, blk), 0)
    col = lax.broadcasted_iota(jnp.int32, (blk, blk), 1)
    tri_u = jnp.where(row <= col, 1.0, 0.0).astype(BF16)
    incl = _dot(onehot.astype(BF16), tri_u)
    carry = carry_ref[:, 0:1]
    rank = carry + incl - onehot
    r1 = jnp.sum(jnp.where(sel1, rank, 0.0), axis=0, keepdims=True)
    r2 = jnp.sum(jnp.where(sel2, rank, 0.0), axis=0, keepdims=True)
    total = carry + jnp.sum(onehot, axis=1, keepdims=True)
    carry_ref[...] = jnp.broadcast_to(total, carry_ref.shape)
    cum_ref[...] = jnp.broadcast_to(total, cum_ref.shape).astype(jnp.int32)

    ri_ref[0:1, :] = i1
    ri_ref[1:2, :] = i2
    ri_ref[2:3, :] = r1.astype(jnp.int32)
    ri_ref[3:4, :] = r2.astype(jnp.int32)
    rg_ref[0:1, :] = g1
    rg_ref[1:2, :] = g2


def _router(x1, ffn_g, router_w):
    t, d = x1.shape
    n_exp = router_w.shape[1]
    blk = min(TOKEN_BLOCK, t)
    nb = t // blk
    return pl.pallas_call(
        functools.partial(_router_kernel, n_exp=n_exp),
        grid=(nb,),
        in_specs=[
            pl.BlockSpec((blk, d), lambda i: (i, 0)),
            pl.BlockSpec((1, d), lambda i: (0, 0)),
            pl.BlockSpec((n_exp, d), lambda i: (0, 0)),
        ],
        out_specs=[
            pl.BlockSpec((4, blk), lambda i: (0, i)),
            pl.BlockSpec((2, blk), lambda i: (0, i)),
            pl.BlockSpec((None, n_exp, LANES), lambda i: (i, 0, 0)),
        ],
        out_shape=[
            jax.ShapeDtypeStruct((4, t), jnp.int32),
            jax.ShapeDtypeStruct((2, t), F32),
            jax.ShapeDtypeStruct((nb, n_exp, LANES), jnp.int32),
        ],
        scratch_shapes=[pltpu.VMEM((n_exp, LANES), F32)],
        compiler_params=_params(("arbitrary",)),
        name="router",
    )(x1, ffn_g.reshape(1, d), router_w.T)


def _gather_kernel(b0_ref, b1_ref, pos_ref, gate_ref, h_ref, xs_ref, gs_ref, acc_ref, gacc_ref, *, tile, blk):
    i = pl.program_id(0)
    slot = i * tile + lax.broadcasted_iota(jnp.int32, (tile, 1), 0)
    acc_ref[...] = jnp.zeros(acc_ref.shape, F32)
    gacc_ref[...] = jnp.zeros(gacc_ref.shape, F32)

    def body(b, carry):
        pos = pos_ref[b]
        gate = gate_ref[b]
        hit1 = pos[0:1, :] == slot
        hit2 = pos[1:2, :] == slot
        onehot = jnp.where(hit1 | hit2, 1.0, 0.0).astype(BF16)
        start = pl.multiple_of(b * blk, blk)
        acc_ref[...] += _dot(onehot, h_ref[pl.ds(start, blk), :])
        gsel = jnp.where(hit1, gate[0:1, :], 0.0) + jnp.where(hit2, gate[1:2, :], 0.0)
        gacc_ref[...] += jnp.sum(gsel, axis=1, keepdims=True)
        return carry

    lax.fori_loop(b0_ref[i], b1_ref[i], body, 0)
    xs_ref[...] = acc_ref[...].astype(xs_ref.dtype)
    gs_ref[...] = gacc_ref[...]


def _gather(b0, b1, pos3, gate3, h2, n_tiles, tile):
    t, d = h2.shape
    nb, _, blk = pos3.shape
    grid_spec = pltpu.PrefetchScalarGridSpec(
        num_scalar_prefetch=2,
        grid=(n_tiles,),
        in_specs=[
            pl.BlockSpec((nb, 2, blk), lambda i, *_: (0, 0, 0)),
            pl.BlockSpec((nb, 2, blk), lambda i, *_: (0, 0, 0)),
            pl.BlockSpec((t, d), lambda i, *_: (0, 0), pipeline_mode=pl.Buffered(1)),
        ],
        out_specs=[
            pl.BlockSpec((tile, d), lambda i, *_: (i, 0)),
            pl.BlockSpec((tile, 1), lambda i, *_: (i, 0)),
        ],
        scratch_shapes=[pltpu.VMEM((tile, d), F32), pltpu.VMEM((tile, 1), F32)],
    )
    return pl.pallas_call(
        functools.partial(_gather_kernel, tile=tile, blk=blk),
        grid_spec=grid_spec,
        out_shape=[
            jax.ShapeDtypeStruct((n_tiles * tile, d), BF16),
            jax.ShapeDtypeStruct((n_tiles * tile, 1), F32),
        ],
        compiler_params=_params(("parallel",)),
        name="moe_gather",
    )(b0, b1, pos3, gate3, h2)


def _expert_kernel(te_ref, nv_ref, xs_ref, gs_ref, wg_ref, wu_ref, wd_ref, y_ref, acc_ref):
    i = pl.program_id(0)
    j = pl.program_id(1)
    last = pl.num_programs(1) - 1
    valid = i < nv_ref[0]

    @pl.when(valid)
    def _():
        x = xs_ref[...]
        act = (_silu(_dot(x, wg_ref[...])) * _dot(x, wu_ref[...])).astype(BF16)
        part = _dot(act, wd_ref[...])

        @pl.when(j == 0)
        def _():
            acc_ref[...] = part

        @pl.when(j > 0)
        def _():
            acc_ref[...] += part

        @pl.when(j == last)
        def _():
            y_ref[...] = (acc_ref[...] * gs_ref[...]).astype(y_ref.dtype)

    @pl.when(jnp.logical_not(valid) & (j == last))
    def _():
        y_ref[...] = jnp.zeros(y_ref.shape, y_ref.dtype)


def _experts(tile_expert, n_valid, xs, gs, wg, wu, wd, tile):
    p, d = xs.shape
    n_tiles = p // tile
    d_ff = wg.shape[2]
    fc = _ff_chunk(d_ff, 896)
    nj = d_ff // fc

    def chunk(i, j, nv):
        return jnp.where(i < nv[0], j, nj - 1)

    grid_spec = pltpu.PrefetchScalarGridSpec(
        num_scalar_prefetch=2,
        grid=(n_tiles, nj),
        in_specs=[
            pl.BlockSpec((tile, d), lambda i, j, te, nv: (i, 0)),
            pl.BlockSpec((tile, 1), lambda i, j, te, nv: (i, 0)),
            pl.BlockSpec((None, d, fc), lambda i, j, te, nv: (te[i], 0, chunk(i, j, nv))),
            pl.BlockSpec((None, d, fc), lambda i, j, te, nv: (te[i], 0, chunk(i, j, nv))),
            pl.BlockSpec((None, fc, d), lambda i, j, te, nv: (te[i], chunk(i, j, nv), 0)),
        ],
        out_specs=pl.BlockSpec((tile, d), lambda i, j, te, nv: (i, 0)),
        scratch_shapes=[pltpu.VMEM((tile, d), F32)],
    )
    return pl.pallas_call(
        _expert_kernel,
        grid_spec=grid_spec,
        out_shape=jax.ShapeDtypeStruct((p, d), BF16),
        compiler_params=_params(("parallel", "arbitrary")),
        name="moe_experts",
    )(tile_expert, n_valid, xs, gs, wg.astype(BF16), wu.astype(BF16), wd.astype(BF16))


def _combine_kernel(ib_ref, it_ref, fl_ref, ys_ref, pos_ref, x_ref, gfin_ref, o_ref, acc_ref, *, tile, final_norm):
    w = pl.program_id(0)
    flags = fl_ref[w]

    @pl.when((flags & 1) != 0)
    def _():
        acc_ref[...] = x_ref[...]

    @pl.when((flags & 4) != 0)
    def _():
        blk = pos_ref.shape[0]
        slot = it_ref[w] * tile + lax.broadcasted_iota(jnp.int32, (1, tile), 1)
        pos = pos_ref[...]
        hit = (pos[:, 0:1] == slot) | (pos[:, 1:2] == slot)
        acc_ref[...] += _dot(jnp.where(hit, 1.0, 0.0).astype(BF16), ys_ref[...])

    @pl.when((flags & 2) != 0)
    def _():
        out = acc_ref[...]
        o_ref[...] = _rms(out, gfin_ref[...]) if final_norm else out


def _combine(item_block, item_tile, item_flags, ys, pos_col, x1, final_g, tile, final_norm):
    t, d = x1.shape
    blk = min(TOKEN_BLOCK, t)
    n_items = item_block.shape[0]
    grid_spec = pltpu.PrefetchScalarGridSpec(
        num_scalar_prefetch=3,
        grid=(n_items,),
        in_specs=[
            pl.BlockSpec((tile, d), lambda w, ib, it, fl: (it[w], 0)),
            pl.BlockSpec((blk, 2), lambda w, ib, it, fl: (ib[w], 0)),
            pl.BlockSpec((blk, d), lambda w, ib, it, fl: (ib[w], 0)),
            pl.BlockSpec((1, d), lambda w, ib, it, fl: (0, 0)),
        ],
        out_specs=pl.BlockSpec((blk, d), lambda w, ib, it, fl: (ib[w], 0)),
        scratch_shapes=[pltpu.VMEM((blk, d), F32)],
    )
    return pl.pallas_call(
        functools.partial(_combine_kernel, tile=tile, final_norm=final_norm),
        grid_spec=grid_spec,
        out_shape=jax.ShapeDtypeStruct((t, d), F32),
        compiler_params=_params(("arbitrary",)),
        name="moe_combine",
    )(item_block, item_tile, item_flags, ys, pos_col, x1, final_g.reshape(1, d))


def _moe(x1, h2, ffn_g, router_w, wg, wu, wd, final_g, final_norm):
    t, d = x1.shape
    n_exp = router_w.shape[1]
    blk = min(TOKEN_BLOCK, t)
    nb = t // blk
    tile = min(EXPERT_TILE, t)
    n_tiles = TOP_K * t // tile + n_exp
    i32 = jnp.int32

    ri, rg, cum_blocks = _router(x1, ffn_g, router_w)

    cum = jnp.concatenate([jnp.zeros((1, n_exp), i32), cum_blocks[:, :, 0]], axis=0)
    counts = cum[nb]
    tiles_per = (counts + tile - 1) // tile
    tile_end = jnp.cumsum(tiles_per)
    tile_start = tile_end - tiles_per
    seg_start = tile_start * tile
    n_valid = tile_end[n_exp - 1]
    tile_ids = jnp.arange(n_tiles, dtype=i32)
    tile_expert = jnp.minimum(jnp.sum(tile_ids[:, None] >= tile_end[None, :], axis=1), n_exp - 1).astype(i32)
    tile_expert = jnp.where(tile_ids < n_valid, tile_expert, tile_expert[jnp.maximum(n_valid - 1, 0)])

    pos = jnp.take(seg_start, ri[0:2], axis=0) + ri[2:4]
    pos3 = pos.reshape(2, nb, blk).transpose(1, 0, 2)
    gate3 = rg.reshape(2, nb, blk).transpose(1, 0, 2)

    r0 = (tile_ids - tile_start[tile_expert]) * tile
    cum_e = cum[:, tile_expert]
    b0 = jnp.sum(cum_e[1:] <= r0[None, :], axis=0)
    b1 = jnp.sum(cum_e[:-1] < (r0 + tile)[None, :], axis=0)
    live = tile_ids < n_valid
    b0 = jnp.where(live, b0, 0).astype(i32)
    b1 = jnp.where(live, b1, 0).astype(i32)

    xs, gs = _gather(b0, b1, pos3, gate3, h2, n_tiles, tile)
    ys = _experts(tile_expert, n_valid.reshape(1).astype(i32), xs, gs, wg, wu, wd, tile)

    lo = seg_start[None, :] + cum[:-1]
    hi = seg_start[None, :] + cum[1:]
    first_tile = lo // tile
    n_pair = jnp.where(hi > lo, (hi - 1) // tile - first_tile + 1, 0).reshape(-1)
    pair_end = jnp.cumsum(n_pair)
    pair_start = pair_end - n_pair
    n_items = nb * n_exp + n_tiles
    total = pair_end[-1]
    w_ids = jnp.arange(n_items, dtype=i32)
    w_clamped = jnp.minimum(w_ids, total - 1)
    pair = jnp.sum(w_clamped[:, None] >= pair_end[None, :], axis=1)
    item_block = (pair // n_exp).astype(i32)
    item_tile = (first_tile.reshape(-1)[pair] + w_clamped - pair_start[pair]).astype(i32)
    valid = w_ids < total
    prev_block = jnp.concatenate([jnp.full((1,), -1, i32), item_block[:-1]])
    next_block = jnp.concatenate([item_block[1:], jnp.full((1,), -1, i32)])
    is_first = valid & (item_block != prev_block)
    is_last = valid & ((item_block != next_block) | (w_ids == total - 1))
    item_flags = (is_first * 1 + is_last * 2 + valid * 4).astype(i32)

    return _combine(item_block, item_tile, item_flags, ys, pos.T, x1, final_g, tile, final_norm)


def kernel(x, mix_norm_g, w_in, conv_w, conv_b, dt_bias, a_log, d_skip, attn_norm_g, ssd_norm_g, w_out, ffn_norm_g, dense_w_gate, dense_w_up, dense_w_down, router_w, moe_w_gate, moe_w_up, moe_w_down, final_norm_g):
    b, s, d = x.shape
    depth = w_in.shape[0]
    attn_width = attn_norm_g.shape[1]
    heads = dt_bias.shape[1]
    n_main = w_in.shape[2] - heads
    x2 = x.reshape(b * s, d)
    for layer in range(depth):
        proj, dtc, dtr = _inproj(x2, mix_norm_g[layer], w_in[layer, :, :n_main], w_in[layer, :, n_main:])
        proj3 = proj.reshape(b, s, n_main)
        attn = _attention(proj3, attn_width)
        y = _ssd(proj3, dtc.reshape(b, s, heads), dtr, conv_w[layer], conv_b[layer], dt_bias[layer],
                 a_log[layer], d_skip[layer], ssd_norm_g[layer], attn_width)
        x1, h2 = _outproj(attn.reshape(b * s, attn_width), y.reshape(b * s, -1), x2, attn_norm_g[layer],
                          w_out[layer], ffn_norm_g[layer])
        final = layer == depth - 1
        i = layer // 2
        if layer % 2 == 0:
            x2 = _dense_ffn(h2, x1, dense_w_gate[i], dense_w_up[i], dense_w_down[i], final_norm_g, final)
        else:
            x2 = _moe(x1, h2, ffn_norm_g[layer], router_w[i], moe_w_gate[i], moe_w_up[i], moe_w_down[i],
                      final_norm_g, final)
    return x2.reshape(b, s, d)
```

```python
import functools

import jax
import jax.numpy as jnp
from jax import lax
from jax.experimental import pallas as pl
from jax.experimental.pallas import tpu as pltpu

F32 = jnp.float32
BF16 = jnp.bfloat16
HIGHEST = lax.Precision.HIGHEST

HEAD_DIM = 64
HEADS_PER_VREG = 2
LANES = 128
SSD_GROUPS = 2
SSD_STATE = 128
CONV_K = 4
CONV_HALO = 8
TOP_K = 2
EPS = 1e-5
NEG_BIG = -1e30
LOG2_E = 1.4426950408889634

ROW_TILE = 512
ATTN_TILE = 256
SSD_CHUNK = 256
TOKEN_BLOCK = 512
EXPERT_TILE = 512
VMEM_LIMIT = 56 * 1024 * 1024


def _params(semantics):
    return pltpu.CompilerParams(dimension_semantics=semantics, vmem_limit_bytes=VMEM_LIMIT)


def _rms(x, g):
    var = jnp.mean(x * x, axis=-1, keepdims=True)
    return x * lax.rsqrt(var + EPS) * g


def _silu(x):
    return x / (1.0 + jnp.exp(-x))


def _softplus(x):
    return jnp.maximum(x, 0.0) + jnp.log1p(jnp.exp(-jnp.abs(x)))


def _dot(a, b):
    return jnp.dot(a, b, preferred_element_type=F32)


def _dot_nt(a, b, precision=None):
    return lax.dot_general(a, b, (((1,), (1,)), ((), ())), precision=precision,
                           preferred_element_type=F32)


def _inproj_kernel(x_ref, g_ref, w_ref, proj_ref, dtc_ref, dtr_ref):
    h = _rms(x_ref[...], g_ref[...])
    res = _dot(h.astype(BF16), w_ref[...])
    n = proj_ref.shape[-1]
    n_dt = dtc_ref.shape[-1]
    proj_ref[...] = res[:, :n].astype(proj_ref.dtype)
    dt_raw = res[:, n:]
    dtc_ref[...] = dt_raw[:, :n_dt]
    dtr_ref[...] = dt_raw.T[:n_dt, :]


def _inproj(x2, g, w_main, w_dt):
    t, d = x2.shape
    n = w_main.shape[1]
    nh = w_dt.shape[1]
    w_all = jnp.concatenate([w_main, jnp.pad(w_dt, ((0, 0), (0, LANES - nh)))], axis=1).astype(BF16)
    tm = min(ROW_TILE, t)
    return pl.pallas_call(
        _inproj_kernel,
        grid=(t // tm,),
        in_specs=[
            pl.BlockSpec((tm, d), lambda i: (i, 0)),
            pl.BlockSpec((1, d), lambda i: (0, 0)),
            pl.BlockSpec((d, n + LANES), lambda i: (0, 0)),
        ],
        out_specs=[
            pl.BlockSpec((tm, n), lambda i: (i, 0)),
            pl.BlockSpec((tm, nh), lambda i: (i, 0)),
            pl.BlockSpec((nh, tm), lambda i: (0, i)),
        ],
        out_shape=[
            jax.ShapeDtypeStruct((t, n), BF16),
            jax.ShapeDtypeStruct((t, nh), F32),
            jax.ShapeDtypeStruct((nh, t), F32),
        ],
        compiler_params=_params(("parallel",)),
        name="inproj",
    )(x2, g.reshape(1, d), w_all)


def _attn_kernel(q_ref, k_ref, v_ref, o_ref, *, tile):
    i = pl.program_id(2)
    lane = lax.broadcasted_iota(jnp.int32, (tile, LANES), 1)
    row = lax.broadcasted_iota(jnp.int32, (tile, tile), 0)
    col = lax.broadcasted_iota(jnp.int32, (tile, tile), 1)
    m_incl = jnp.where(row >= col, 1.0, 0.0).astype(BF16)
    causal = col < row
    q_all = q_ref[...].astype(F32) * (HEAD_DIM ** -0.5 * LOG2_E)
    head_lanes = [(lane >= hh * HEAD_DIM) & (lane < (hh + 1) * HEAD_DIM) for hh in range(HEADS_PER_VREG)]
    q_heads = [jnp.where(m, q_all, 0.0).astype(BF16) for m in head_lanes]

    def block(j, state, masked):
        carries, acc = state
        start = pl.multiple_of(j * tile, tile)
        ks = k_ref[pl.ds(start, tile), :]
        vs = v_ref[pl.ds(start, tile), :]
        new_carries, ws, v_heads = [], [], []
        for qh, m, carry in zip(q_heads, head_lanes, carries):
            z2 = _dot_nt(qh, ks)
            sp2 = jnp.maximum(z2, 0.0) + jnp.log2(1.0 + jnp.exp2(-jnp.abs(z2)))
            if masked:
                sp2 = jnp.where(causal, sp2, 0.0)
            cs2 = _dot(sp2.astype(BF16), m_incl) + carry
            w = jnp.exp2(z2 - cs2)
            if masked:
                w = jnp.where(causal, w, 0.0)
            new_carries.append(cs2[:, 0:1])
            ws.append(w.astype(BF16))
            v_heads.append(jnp.where(m, vs, jnp.zeros_like(vs)))
        acc = acc + _dot(jnp.concatenate(ws, axis=1), jnp.concatenate(v_heads, axis=0))
        return tuple(new_carries), acc

    state = ((jnp.zeros((tile, 1), F32),) * HEADS_PER_VREG, jnp.zeros((tile, LANES), F32))
    state = block(i, state, True)
    state = lax.cond(i % 2 == 1, lambda st: block(i - 1, st, False), lambda st: st, state)
    top = i - 1 - i % 2

    def two_blocks(pp, st):
        j = top - 2 * pp
        return block(j - 1, block(j, st, False), False)

    _, acc = lax.fori_loop(0, i // 2, two_blocks, state)
    o_ref[...] = acc.astype(o_ref.dtype)


def _attention(proj3, attn_width):
    b, s, _ = proj3.shape
    tile = min(ATTN_TILE, s)
    n_pairs = attn_width // LANES
    return pl.pallas_call(
        functools.partial(_attn_kernel, tile=tile),
        grid=(b, n_pairs, s // tile),
        in_specs=[
            pl.BlockSpec((None, tile, LANES), lambda bi, p, i: (bi, i, p)),
            pl.BlockSpec((None, s, LANES), lambda bi, p, i: (bi, 0, n_pairs + p)),
            pl.BlockSpec((None, s, LANES), lambda bi, p, i: (bi, 0, 2 * n_pairs + p)),
        ],
        out_specs=pl.BlockSpec((None, tile, LANES), lambda bi, p, i: (bi, i, p)),
        out_shape=jax.ShapeDtypeStruct((b, s, attn_width), BF16),
        compiler_params=_params(("parallel", "parallel", "arbitrary")),
        name="sb_attention",
    )(proj3, proj3, proj3)


def _ssd_kernel(xbc_ref, z_ref, dtc_ref, dtr_ref, cw_ref, cb_ref, dtb_c_ref, dtb_r_ref, alog_c_ref,
                alog_r_ref, dskip_ref, gn_ref, y_ref, ext_ref, state_ref, *, chunk, width, heads):
    c = pl.program_id(1)
    heads_per_group = heads // SSD_GROUPS
    group_width = heads_per_group * HEAD_DIM
    gs = SSD_GROUPS * SSD_STATE

    @pl.when(c == 0)
    def _():
        ext_ref[0:CONV_HALO, :] = jnp.zeros((CONV_HALO, ext_ref.shape[1]), F32)
        state_ref[...] = jnp.zeros(state_ref.shape, F32)

    ext_ref[CONV_HALO:CONV_HALO + chunk, :] = xbc_ref[...].astype(F32)
    conv = cb_ref[...]
    for k in range(CONV_K):
        conv = conv + cw_ref[k:k + 1, :] * ext_ref[pl.ds(CONV_HALO - (CONV_K - 1) + k, chunk), :]
    ext_ref[0:CONV_HALO, :] = ext_ref[chunk:chunk + CONV_HALO, :]
    xbc = _silu(conv)
    xs = xbc[:, :width]
    bm = xbc[:, width:width + gs]
    cm = xbc[:, width + gs:]

    dt_c = _softplus(dtc_ref[...] + dtb_c_ref[...])
    dt_r = _softplus(dtr_ref[...] + dtb_r_ref[...])
    da_c = dt_c * (-jnp.exp(alog_c_ref[...]))
    da_r = dt_r * (-jnp.exp(alog_r_ref[...]))
    row = lax.broadcasted_iota(jnp.int32, (chunk, chunk), 0)
    col = lax.broadcasted_iota(jnp.int32, (chunk, chunk), 1)
    lower = row >= col
    tri_l = jnp.where(lower, 1.0, 0.0)
    tri_u = jnp.where(row <= col, 1.0, 0.0)
    cum_c = jnp.dot(tri_l, da_c, precision=HIGHEST, preferred_element_type=F32)
    cum_r = jnp.dot(da_r, tri_u, precision=HIGHEST, preferred_element_type=F32)
    last_c = cum_c[chunk - 1:chunk, :]
    last_r = cum_r[:, chunk - 1:chunk]
    tail_r = jnp.exp(last_r - cum_r)

    lane = lax.broadcasted_iota(jnp.int32, (chunk, LANES), 1)

    def per_head_lanes(a):
        first = lane[:a.shape[0]] < HEAD_DIM
        return jnp.concatenate([jnp.where(first, a[:, p:p + 1], a[:, p + 1:p + 2])
                                for p in range(0, heads, HEADS_PER_VREG)], axis=1)

    dt_x = per_head_lanes(dt_c)
    head_x = per_head_lanes(jnp.exp(cum_c))
    last_x = per_head_lanes(jnp.exp(last_c))
    xdt = xs * dt_x

    y_parts = []
    new_state = []
    for g in range(SSD_GROUPS):
        b_g = bm[:, g * SSD_STATE:(g + 1) * SSD_STATE]
        c_g = cm[:, g * SSD_STATE:(g + 1) * SSD_STATE]
        cb = _dot_nt(c_g.astype(BF16), b_g.astype(BF16))
        bt_g = b_g.T
        st_g = state_ref[:, g * group_width:(g + 1) * group_width]
        y_off = _dot(c_g.astype(BF16), st_g.astype(BF16))
        for p in range(group_width // LANES):
            lo = g * group_width + p * LANES
            xdt_p = xdt[:, lo:lo + LANES]
            y_p = jnp.zeros((chunk, LANES), F32)
            s_p = jnp.zeros((SSD_STATE, LANES), F32)
            for hh in range(HEADS_PER_VREG):
                h = lo // HEAD_DIM + hh
                head_lanes = (lane >= hh * HEAD_DIM) & (lane < (hh + 1) * HEAD_DIM)
                x_h = jnp.where(head_lanes, xdt_p, 0.0).astype(BF16)
                seg = cum_c[:, h:h + 1] - cum_r[h:h + 1, :]
                decay = jnp.exp(jnp.where(lower, seg, NEG_BIG))
                y_p = y_p + _dot((cb * decay).astype(BF16), x_h)
                s_p = s_p + _dot((bt_g * tail_r[h:h + 1, :]).astype(BF16), x_h)
            y_parts.append(y_p + y_off[:, p * LANES:(p + 1) * LANES] * head_x[:, lo:lo + LANES])
            new_state.append(s_p)
    y = jnp.concatenate(y_parts, axis=1) + dskip_ref[...] * xs
    state_ref[...] = state_ref[...] * last_x + jnp.concatenate(new_state, axis=1)

    y = y * _silu(z_ref[...].astype(F32))
    normed = []
    for g in range(SSD_GROUPS):
        yg = y[:, g * group_width:(g + 1) * group_width]
        normed.append(yg * lax.rsqrt(jnp.mean(yg * yg, axis=-1, keepdims=True) + EPS))
    y_ref[...] = (jnp.concatenate(normed, axis=1) * gn_ref[...]).astype(y_ref.dtype)


def _ssd(proj3, dtc3, dtr, conv_w, conv_b, dt_bias, a_log, d_skip, ssd_norm_g, attn_width):
    b, s, _ = proj3.shape
    heads = dt_bias.shape[0]
    width = heads * HEAD_DIM
    conv_dim = conv_w.shape[1]
    chunk = min(SSD_CHUNK, s)
    nc = s // chunk
    z_blk = 3 * attn_width // width
    xbc_blk = (3 * attn_width + width) // conv_dim
    small = lambda shape: pl.BlockSpec(shape, lambda bi, ci: (0, 0))
    return pl.pallas_call(
        functools.partial(_ssd_kernel, chunk=chunk, width=width, heads=heads),
        grid=(b, nc),
        in_specs=[
            pl.BlockSpec((None, chunk, conv_dim), lambda bi, ci: (bi, ci, xbc_blk)),
            pl.BlockSpec((None, chunk, width), lambda bi, ci: (bi, ci, z_blk)),
            pl.BlockSpec((None, chunk, heads), lambda bi, ci: (bi, ci, 0)),
            pl.BlockSpec((heads, chunk), lambda bi, ci: (0, bi * nc + ci)),
            small((CONV_K, conv_dim)), small((1, conv_dim)),
            small((1, heads)), small((heads, 1)), small((1, heads)), small((heads, 1)),
            small((1, width)), small((1, width)),
        ],
        out_specs=pl.BlockSpec((None, chunk, width), lambda bi, ci: (bi, ci, 0)),
        out_shape=jax.ShapeDtypeStruct((b, s, width), BF16),
        scratch_shapes=[
            pltpu.VMEM((chunk + CONV_HALO, conv_dim), F32),
            pltpu.VMEM((SSD_STATE, width), F32),
        ],
        compiler_params=_params(("parallel", "arbitrary")),
        name="conv_ssd",
    )(proj3, proj3, dtc3, dtr, conv_w, conv_b.reshape(1, conv_dim),
      dt_bias.reshape(1, heads), dt_bias.reshape(heads, 1), a_log.reshape(1, heads), a_log.reshape(heads, 1),
      jnp.repeat(d_skip, HEAD_DIM).reshape(1, width), ssd_norm_g.reshape(1, width))


def _outproj_kernel(attn_ref, y_ref, x_ref, ga_ref, wa_ref, wy_ref, gf_ref, x1_ref, h_ref):
    a = _rms(attn_ref[...].astype(F32), ga_ref[...]).astype(BF16)
    x1 = x_ref[...] + _dot(a, wa_ref[...]) + _dot(y_ref[...], wy_ref[...])
    x1_ref[...] = x1
    h_ref[...] = _rms(x1, gf_ref[...]).astype(h_ref.dtype)


def _outproj(attn2, y2, x2, attn_norm_g, w_out, ffn_norm_g):
    t, d = x2.shape
    wa = attn2.shape[1]
    wy = y2.shape[1]
    tm = min(ROW_TILE, t)
    w_bf = w_out.astype(BF16)
    return pl.pallas_call(
        _outproj_kernel,
        grid=(t // tm,),
        in_specs=[
            pl.BlockSpec((tm, wa), lambda i: (i, 0)),
            pl.BlockSpec((tm, wy), lambda i: (i, 0)),
            pl.BlockSpec((tm, d), lambda i: (i, 0)),
            pl.BlockSpec((1, wa), lambda i: (0, 0)),
            pl.BlockSpec((wa, d), lambda i: (0, 0)),
            pl.BlockSpec((wy, d), lambda i: (0, 0)),
            pl.BlockSpec((1, d), lambda i: (0, 0)),
        ],
        out_specs=[pl.BlockSpec((tm, d), lambda i: (i, 0)), pl.BlockSpec((tm, d), lambda i: (i, 0))],
        out_shape=[jax.ShapeDtypeStruct((t, d), F32), jax.ShapeDtypeStruct((t, d), BF16)],
        compiler_params=_params(("parallel",)),
        name="outproj",
    )(attn2, y2, x2, attn_norm_g.reshape(1, wa), w_bf[:wa], w_bf[wa:], ffn_norm_g.reshape(1, d))


def _dense_ffn_kernel(h_ref, x_ref, wg_ref, wu_ref, wd_ref, gfin_ref, o_ref, acc_ref, *, final_norm):
    j = pl.program_id(1)
    h = h_ref[...]
    act = (_silu(_dot(h, wg_ref[...])) * _dot(h, wu_ref[...])).astype(BF16)
    part = _dot(act, wd_ref[...])

    @pl.when(j == 0)
    def _():
        acc_ref[...] = x_ref[...] + part

    @pl.when(j > 0)
    def _():
        acc_ref[...] += part

    @pl.when(j == pl.num_programs(1) - 1)
    def _():
        out = acc_ref[...]
        o_ref[...] = _rms(out, gfin_ref[...]) if final_norm else out


def _ff_chunk(d_ff, target):
    best = LANES
    for c in range(LANES, min(d_ff, target) + 1, LANES):
        if d_ff % c == 0:
            best = c
    return best


def _dense_ffn(h2, x1, wg, wu, wd, final_g, final_norm):
    t, d = x1.shape
    d_ff = wg.shape[1]
    tm = min(ROW_TILE, t)
    fc = _ff_chunk(d_ff, 1408)
    return pl.pallas_call(
        functools.partial(_dense_ffn_kernel, final_norm=final_norm),
        grid=(t // tm, d_ff // fc),
        in_specs=[
            pl.BlockSpec((tm, d), lambda i, j: (i, 0)),
            pl.BlockSpec((tm, d), lambda i, j: (i, 0)),
            pl.BlockSpec((d, fc), lambda i, j: (0, j)),
            pl.BlockSpec((d, fc), lambda i, j: (0, j)),
            pl.BlockSpec((fc, d), lambda i, j: (j, 0)),
            pl.BlockSpec((1, d), lambda i, j: (0, 0)),
        ],
        out_specs=pl.BlockSpec((tm, d), lambda i, j: (i, 0)),
        out_shape=jax.ShapeDtypeStruct((t, d), F32),
        scratch_shapes=[pltpu.VMEM((tm, d), F32)],
        compiler_params=_params(("parallel", "arbitrary")),
        name="dense_ffn",
    )(h2, x1, wg.astype(BF16), wu.astype(BF16), wd.astype(BF16), final_g.reshape(1, d))


def _router_kernel(x_ref, g_ref, rwt_ref, ri_ref, rg_ref, cum_ref, carry_ref, *, n_exp):
    blk = x_ref.shape[0]

    @pl.when(pl.program_id(0) == 0)
    def _():
        carry_ref[...] = jnp.zeros(carry_ref.shape, F32)

    h = _rms(x_ref[...], g_ref[...])
    logits = _dot_nt(rwt_ref[...], h, precision=HIGHEST)
    eidx = lax.broadcasted_iota(jnp.int32, (n_exp, blk), 0)
    m1 = jnp.max(logits, axis=0, keepdims=True)
    i1 = jnp.min(jnp.where(logits == m1, eidx, n_exp), axis=0, keepdims=True)
    rest = jnp.where(eidx == i1, -jnp.inf, logits)
    m2 = jnp.max(rest, axis=0, keepdims=True)
    i2 = jnp.min(jnp.where(rest == m2, eidx, n_exp), axis=0, keepdims=True)
    e21 = jnp.exp(m2 - m1)
    g1 = 1.0 / (1.0 + e21)
    g2 = e21 * g1

    sel1 = eidx == i1
    sel2 = eidx == i2
    onehot = jnp.where(sel1 | sel2, 1.0, 0.0)
    row = lax.broadcasted_iota(jnp.int32, (blk, blk), 0)
    col = lax.broadcasted_iota(jnp.int32, (blk, blk), 1)
    tri_u = jnp.where(row <= col, 1.0, 0.0).astype(BF16)
    incl = _dot(onehot.astype(BF16), tri_u)
    carry = carry_ref[:, 0:1]
    rank = carry + incl - onehot
    r1 = jnp.sum(jnp.where(sel1, rank, 0.0), axis=0, keepdims=True)
    r2 = jnp.sum(jnp.where(sel2, rank, 0.0), axis=0, keepdims=True)
    total = carry + jnp.sum(onehot, axis=1, keepdims=True)
    carry_ref[...] = jnp.broadcast_to(total, carry_ref.shape)
    cum_ref[...] = jnp.broadcast_to(total, cum_ref.shape).astype(jnp.int32)

    ri_ref[0:1, :] = i1
    ri_ref[1:2, :] = i2
    ri_ref[2:3, :] = r1.astype(jnp.int32)
    ri_ref[3:4, :] = r2.astype(jnp.int32)
    rg_ref[0:1, :] = g1
    rg_ref[1:2, :] = g2


def _router(x1, ffn_g, router_w):
    t, d = x1.shape
    n_exp = router_w.shape[1]
    blk = min(TOKEN_BLOCK, t)
    nb = t // blk
    return pl.pallas_call(
        functools.partial(_router_kernel, n_exp=n_exp),
        grid=(nb,),
        in_specs=[
            pl.BlockSpec((blk, d), lambda i: (i, 0)),
            pl.BlockSpec((1, d), lambda i: (0, 0)),
            pl.BlockSpec((n_exp, d), lambda i: (0, 0)),
        ],
        out_specs=[
            pl.BlockSpec((4, blk), lambda i: (0, i)),
            pl.BlockSpec((2, blk), lambda i: (0, i)),
            pl.BlockSpec((None, n_exp, LANES), lambda i: (i, 0, 0)),
        ],
        out_shape=[
            jax.ShapeDtypeStruct((4, t), jnp.int32),
            jax.ShapeDtypeStruct((2, t), F32),
            jax.ShapeDtypeStruct((nb, n_exp, LANES), jnp.int32),
        ],
        scratch_shapes=[pltpu.VMEM((n_exp, LANES), F32)],
        compiler_params=_params(("arbitrary",)),
        name="router",
    )(x1, ffn_g.reshape(1, d), router_w.T)


def _gather_kernel(b0_ref, b1_ref, pos_ref, gate_ref, h_ref, xs_ref, gs_ref, acc_ref, gacc_ref, *, tile, blk):
    i = pl.program_id(0)
    slot = i * tile + lax.broadcasted_iota(jnp.int32, (tile, 1), 0)
    acc_ref[...] = jnp.zeros(acc_ref.shape, F32)
    gacc_ref[...] = jnp.zeros(gacc_ref.shape, F32)

    def body(b, carry):
        pos = pos_ref[b]
        gate = gate_ref[b]
        hit1 = pos[0:1, :] == slot
        hit2 = pos[1:2, :] == slot
        onehot = jnp.where(hit1 | hit2, 1.0, 0.0).astype(BF16)
        start = pl.multiple_of(b * blk, blk)
        acc_ref[...] += _dot(onehot, h_ref[pl.ds(start, blk), :])
        gsel = jnp.where(hit1, gate[0:1, :], 0.0) + jnp.where(hit2, gate[1:2, :], 0.0)
        gacc_ref[...] += jnp.sum(gsel, axis=1, keepdims=True)
        return carry

    lax.fori_loop(b0_ref[i], b1_ref[i], body, 0)
    xs_ref[...] = acc_ref[...].astype(xs_ref.dtype)
    gs_ref[...] = gacc_ref[...]


def _gather(b0, b1, pos3, gate3, h2, n_tiles, tile):
    t, d = h2.shape
    nb, _, blk = pos3.shape
    grid_spec = pltpu.PrefetchScalarGridSpec(
        num_scalar_prefetch=2,
        grid=(n_tiles,),
        in_specs=[
            pl.BlockSpec((nb, 2, blk), lambda i, *_: (0, 0, 0)),
            pl.BlockSpec((nb, 2, blk), lambda i, *_: (0, 0, 0)),
            pl.BlockSpec((t, d), lambda i, *_: (0, 0), pipeline_mode=pl.Buffered(1)),
        ],
        out_specs=[
            pl.BlockSpec((tile, d), lambda i, *_: (i, 0)),
            pl.BlockSpec((tile, 1), lambda i, *_: (i, 0)),
        ],
        scratch_shapes=[pltpu.VMEM((tile, d), F32), pltpu.VMEM((tile, 1), F32)],
    )
    return pl.pallas_call(
        functools.partial(_gather_kernel, tile=tile, blk=blk),
        grid_spec=grid_spec,
        out_shape=[
            jax.ShapeDtypeStruct((n_tiles * tile, d), BF16),
            jax.ShapeDtypeStruct((n_tiles * tile, 1), F32),
        ],
        compiler_params=_params(("parallel",)),
        name="moe_gather",
    )(b0, b1, pos3, gate3, h2)


def _expert_kernel(te_ref, nv_ref, xs_ref, gs_ref, wg_ref, wu_ref, wd_ref, y_ref, acc_ref):
    i = pl.program_id(0)
    j = pl.program_id(1)
    last = pl.num_programs(1) - 1
    valid = i < nv_ref[0]

    @pl.when(valid)
    def _():
        x = xs_ref[...]
        act = (_silu(_dot(x, wg_ref[...])) * _dot(x, wu_ref[...])).astype(BF16)
        part = _dot(act, wd_ref[...])

        @pl.when(j == 0)
        def _():
            acc_ref[...] = part

        @pl.when(j > 0)
        def _():
            acc_ref[...] += part

        @pl.when(j == last)
        def _():
            y_ref[...] = (acc_ref[...] * gs_ref[...]).astype(y_ref.dtype)

    @pl.when(jnp.logical_not(valid) & (j == last))
    def _():
        y_ref[...] = jnp.zeros(y_ref.shape, y_ref.dtype)


def _experts(tile_expert, n_valid, xs, gs, wg, wu, wd, tile):
    p, d = xs.shape
    n_tiles = p // tile
    d_ff = wg.shape[2]
    fc = _ff_chunk(d_ff, 1792)
    nj = d_ff // fc

    def chunk(i, j, nv):
        return jnp.where(i < nv[0], j, nj - 1)

    grid_spec = pltpu.PrefetchScalarGridSpec(
        num_scalar_prefetch=2,
        grid=(n_tiles, nj),
        in_specs=[
            pl.BlockSpec((tile, d), lambda i, j, te, nv: (i, 0)),
            pl.BlockSpec((tile, 1), lambda i, j, te, nv: (i, 0)),
            pl.BlockSpec((None, d, fc), lambda i, j, te, nv: (te[i], 0, chunk(i, j, nv))),
            pl.BlockSpec((None, d, fc), lambda i, j, te, nv: (te[i], 0, chunk(i, j, nv))),
            pl.BlockSpec((None, fc, d), lambda i, j, te, nv: (te[i], chunk(i, j, nv), 0)),
        ],
        out_specs=pl.BlockSpec((tile, d), lambda i, j, te, nv: (i, 0)),
        scratch_shapes=[pltpu.VMEM((tile, d), F32)],
    )
    return pl.pallas_call(
        _expert_kernel,
        grid_spec=grid_spec,
        out_shape=jax.ShapeDtypeStruct((p, d), BF16),
        compiler_params=_params(("parallel", "arbitrary")),
        name="moe_experts",
    )(tile_expert, n_valid, xs, gs, wg.astype(BF16), wu.astype(BF16), wd.astype(BF16))


def _combine_kernel(ib_ref, it_ref, fl_ref, ys_ref, pos_ref, x_ref, gfin_ref, o_ref, acc_ref, *, tile, final_norm):
    w = pl.program_id(0)
    flags = fl_ref[w]

    @pl.when((flags & 1) != 0)
    def _():
        acc_ref[...] = x_ref[...]

    @pl.when((flags & 4) != 0)
    def _():
        blk = pos_ref.shape[0]
        slot = it_ref[w] * tile + lax.broadcasted_iota(jnp.int32, (1, tile), 1)
        pos = pos_ref[...]
        hit = (pos[:, 0:1] == slot) | (pos[:, 1:2] == slot)
        acc_ref[...] += _dot(jnp.where(hit, 1.0, 0.0).astype(BF16), ys_ref[...])

    @pl.when((flags & 2) != 0)
    def _():
        out = acc_ref[...]
        o_ref[...] = _rms(out, gfin_ref[...]) if final_norm else out


def _combine(item_block, item_tile, item_flags, ys, pos_col, x1, final_g, tile, final_norm):
    t, d = x1.shape
    blk = min(TOKEN_BLOCK, t)
    n_items = item_block.shape[0]
    grid_spec = pltpu.PrefetchScalarGridSpec(
        num_scalar_prefetch=3,
        grid=(n_items,),
        in_specs=[
            pl.BlockSpec((tile, d), lambda w, ib, it, fl: (it[w], 0)),
            pl.BlockSpec((blk, 2), lambda w, ib, it, fl: (ib[w], 0)),
            pl.BlockSpec((blk, d), lambda w, ib, it, fl: (ib[w], 0)),
            pl.BlockSpec((1, d), lambda w, ib, it, fl: (0, 0)),
        ],
        out_specs=pl.BlockSpec((blk, d), lambda w, ib, it, fl: (ib[w], 0)),
        scratch_shapes=[pltpu.VMEM((blk, d), F32)],
    )
    return pl.pallas_call(
        functools.partial(_combine_kernel, tile=tile, final_norm=final_norm),
        grid_spec=grid_spec,
        out_shape=jax.ShapeDtypeStruct((t, d), F32),
        compiler_params=_params(("arbitrary",)),
        name="moe_combine",
    )(item_block, item_tile, item_flags, ys, pos_col, x1, final_g.reshape(1, d))


def _moe(x1, h2, ffn_g, router_w, wg, wu, wd, final_g, final_norm):
    t, d = x1.shape
    n_exp = router_w.shape[1]
    blk = min(TOKEN_BLOCK, t)
    nb = t // blk
    tile = min(EXPERT_TILE, t)
    n_tiles = TOP_K * t // tile + n_exp
    i32 = jnp.int32

    ri, rg, cum_blocks = _router(x1, ffn_g, router_w)

    cum = jnp.concatenate([jnp.zeros((1, n_exp), i32), cum_blocks[:, :, 0]], axis=0)
    counts = cum[nb]
    tiles_per = (counts + tile - 1) // tile
    tile_end = jnp.cumsum(tiles_per)
    tile_start = tile_end - tiles_per
    seg_start = tile_start * tile
    n_valid = tile_end[n_exp - 1]
    tile_ids = jnp.arange(n_tiles, dtype=i32)
    tile_expert = jnp.minimum(jnp.sum(tile_ids[:, None] >= tile_end[None, :], axis=1), n_exp - 1).astype(i32)
    tile_expert = jnp.where(tile_ids < n_valid, tile_expert, tile_expert[jnp.maximum(n_valid - 1, 0)])

    seg_of = jnp.sum(jnp.where(ri[0:2, :, None] == jnp.arange(n_exp, dtype=i32), seg_start, 0), axis=-1)
    pos = seg_of + ri[2:4]
    pos3 = pos.reshape(2, nb, blk).transpose(1, 0, 2)
    gate3 = rg.reshape(2, nb, blk).transpose(1, 0, 2)

    r0 = (tile_ids - tile_start[tile_expert]) * tile
    cum_e = cum[:, tile_expert]
    b0 = jnp.sum(cum_e[1:] <= r0[None, :], axis=0)
    b1 = jnp.sum(cum_e[:-1] < (r0 + tile)[None, :], axis=0)
    live = tile_ids < n_valid
    b0 = jnp.where(live, b0, 0).astype(i32)
    b1 = jnp.where(live, b1, 0).astype(i32)

    xs, gs = _gather(b0, b1, pos3, gate3, h2, n_tiles, tile)
    ys = _experts(tile_expert, n_valid.reshape(1).astype(i32), xs, gs, wg, wu, wd, tile)

    lo = seg_start[None, :] + cum[:-1]
    hi = seg_start[None, :] + cum[1:]
    first_tile = lo // tile
    n_pair = jnp.where(hi > lo, (hi - 1) // tile - first_tile + 1, 0).reshape(-1)
    pair_end = jnp.cumsum(n_pair)
    pair_start = pair_end - n_pair
    n_items = nb * n_exp + n_tiles
    total = pair_end[-1]
    w_ids = jnp.arange(n_items, dtype=i32)
    w_clamped = jnp.minimum(w_ids, total - 1)
    pair = jnp.sum(w_clamped[:, None] >= pair_end[None, :], axis=1)
    item_block = (pair // n_exp).astype(i32)
    item_tile = (first_tile.reshape(-1)[pair] + w_clamped - pair_start[pair]).astype(i32)
    valid = w_ids < total
    prev_block = jnp.concatenate([jnp.full((1,), -1, i32), item_block[:-1]])
    next_block = jnp.concatenate([item_block[1:], jnp.full((1,), -1, i32)])
    is_first = valid & (item_block != prev_block)
    is_last = valid & ((item_block != next_block) | (w_ids == total - 1))
    item_flags = (is_first * 1 + is_last * 2 + valid * 4).astype(i32)

    return _combine(item_block, item_tile, item_flags, ys, pos.T, x1, final_g, tile, final_norm)


def kernel(x, mix_norm_g, w_in, conv_w, conv_b, dt_bias, a_log, d_skip, attn_norm_g, ssd_norm_g, w_out, ffn_norm_g, dense_w_gate, dense_w_up, dense_w_down, router_w, moe_w_gate, moe_w_up, moe_w_down, final_norm_g):
    b, s, d = x.shape
    depth = w_in.shape[0]
    attn_width = attn_norm_g.shape[1]
    heads = dt_bias.shape[1]
    n_main = w_in.shape[2] - heads
    x2 = x.reshape(b * s, d)
    for layer in range(depth):
        proj, dtc, dtr = _inproj(x2, mix_norm_g[layer], w_in[layer, :, :n_main], w_in[layer, :, n_main:])
        proj3 = proj.reshape(b, s, n_main)
        attn = _attention(proj3, attn_width)
        y = _ssd(proj3, dtc.reshape(b, s, heads), dtr, conv_w[layer], conv_b[layer], dt_bias[layer],
                 a_log[layer], d_skip[layer], ssd_norm_g[layer], attn_width)
        x1, h2 = _outproj(attn.reshape(b * s, attn_width), y.reshape(b * s, -1), x2, attn_norm_g[layer],
                          w_out[layer], ffn_norm_g[layer])
        final = layer == depth - 1
        i = layer // 2
        if layer % 2 == 0:
            x2 = _dense_ffn(h2, x1, dense_w_gate[i], dense_w_up[i], dense_w_down[i], final_norm_g, final)
        else:
            x2 = _moe(x1, h2, ffn_norm_g[layer], router_w[i], moe_w_gate[i], moe_w_up[i], moe_w_down[i],
                      final_norm_g, final)
    return x2.reshape(b, s, d)
```

```python
import functools

import jax
import jax.numpy as jnp
from jax import lax
from jax.experimental import pallas as pl
from jax.experimental.pallas import tpu as pltpu

F32 = jnp.float32
BF16 = jnp.bfloat16
HIGHEST = lax.Precision.HIGHEST

HEAD_DIM = 64
HEADS_PER_VREG = 2
LANES = 128
MXU_DEPTH = 256
SSD_GROUPS = 2
SSD_STATE = 128
CONV_K = 4
CONV_HALO = 8
TOP_K = 2
EPS = 1e-5
NEG_BIG = -1e30
LOG2_E = 1.4426950408889634
EXP2_CLAMP = 126.0

ROW_TILE = 512
FF_SUB = 256
ATTN_TILE = 256
SSD_CHUNK = 256
TOKEN_BLOCK = 512
EXPERT_TILE = 512
VMEM_LIMIT = 56 * 1024 * 1024


def _params(semantics):
    return pltpu.CompilerParams(dimension_semantics=semantics, vmem_limit_bytes=VMEM_LIMIT)


def _rms(x, g):
    var = jnp.mean(x * x, axis=-1, keepdims=True)
    return x * lax.rsqrt(var + EPS) * g


def _silu(x):
    return x / (1.0 + jnp.exp(-x))


def _softplus(x):
    return jnp.maximum(x, 0.0) + jnp.log1p(jnp.exp(-jnp.abs(x)))


def _dot(a, b):
    return jnp.dot(a, b, preferred_element_type=F32)


def _dot_nt(a, b, precision=None):
    return lax.dot_general(a, b, (((1,), (1,)), ((), ())), precision=precision,
                           preferred_element_type=F32)


def _inproj_kernel(x_ref, g_ref, w_ref, proj_ref, dtc_ref, dtr_ref):
    h = _rms(x_ref[...], g_ref[...])
    res = _dot(h.astype(BF16), w_ref[...])
    n = proj_ref.shape[-1]
    n_dt = dtc_ref.shape[-1]
    proj_ref[...] = res[:, :n].astype(proj_ref.dtype)
    dt_raw = res[:, n:]
    dtc_ref[...] = dt_raw[:, :n_dt]
    dtr_ref[...] = dt_raw.T[:n_dt, :]


def _inproj(x2, g, w_main, w_dt):
    t, d = x2.shape
    n = w_main.shape[1]
    nh = w_dt.shape[1]
    w_all = jnp.concatenate([w_main, jnp.pad(w_dt, ((0, 0), (0, LANES - nh)))], axis=1).astype(BF16)
    tm = min(ROW_TILE, t)
    return pl.pallas_call(
        _inproj_kernel,
        grid=(t // tm,),
        in_specs=[
            pl.BlockSpec((tm, d), lambda i: (i, 0)),
            pl.BlockSpec((1, d), lambda i: (0, 0)),
            pl.BlockSpec((d, n + LANES), lambda i: (0, 0)),
        ],
        out_specs=[
            pl.BlockSpec((tm, n), lambda i: (i, 0)),
            pl.BlockSpec((tm, nh), lambda i: (i, 0)),
            pl.BlockSpec((nh, tm), lambda i: (0, i)),
        ],
        out_shape=[
            jax.ShapeDtypeStruct((t, n), BF16),
            jax.ShapeDtypeStruct((t, nh), F32),
            jax.ShapeDtypeStruct((nh, t), F32),
        ],
        compiler_params=_params(("parallel",)),
        name="inproj",
    )(x2, g.reshape(1, d), w_all)


def _attn_kernel(q_ref, k_ref, v_ref, o_ref, za_ref, zb_ref, wa_ref, wb_ref, *, tile):
    i = pl.program_id(2)
    lane = lax.broadcasted_iota(jnp.int32, (tile, LANES), 1)
    row = lax.broadcasted_iota(jnp.int32, (tile, tile), 0)
    col = lax.broadcasted_iota(jnp.int32, (tile, tile), 1)
    m_incl = jnp.where(row >= col, 1.0, 0.0).astype(BF16)
    causal = col < row
    q_all = q_ref[...].astype(F32) * (HEAD_DIM ** -0.5 * LOG2_E)
    head_lanes = [(lane >= hh * HEAD_DIM) & (lane < (hh + 1) * HEAD_DIM) for hh in range(HEADS_PER_VREG)]
    q_heads = [jnp.where(m, q_all, 0.0).astype(BF16) for m in head_lanes]

    def store_logits(j, z_ref):
        start = pl.multiple_of(jnp.maximum(j, 0) * tile, tile)
        ks = k_ref[pl.ds(start, tile), :]
        for hh, qh in enumerate(q_heads):
            z_ref[hh] = _dot_nt(qh, ks)

    def store_weights(z_ref, w_ref, carries, masked=False, live=None):
        new_carries = []
        for hh, carry in enumerate(carries):
            z2 = z_ref[hh]
            sp2 = jnp.maximum(jnp.log2(1.0 + jnp.exp2(jnp.minimum(z2, EXP2_CLAMP))), z2)
            if masked:
                sp2 = jnp.where(causal, sp2, 0.0)
            cs2 = _dot(sp2.astype(BF16), m_incl) + carry
            w = jnp.exp2(z_ref[hh] - cs2)
            if masked:
                w = jnp.where(causal, w, 0.0)
            w = w.astype(BF16)
            if live is not None:
                w = jnp.where(live, w, jnp.zeros_like(w))
            w_ref[:, hh * tile:(hh + 1) * tile] = w
            new_carries.append(cs2[:, 0:1])
        return tuple(new_carries)

    def weighted_values(w_ref, j):
        start = pl.multiple_of(jnp.maximum(j, 0) * tile, tile)
        vs = v_ref[pl.ds(start, tile), :]
        v_heads = [jnp.where(m, vs, jnp.zeros_like(vs)) for m in head_lanes]
        return _dot(w_ref[...], jnp.concatenate(v_heads, axis=0))

    store_logits(i, za_ref)
    carries = store_weights(za_ref, wa_ref, (jnp.zeros((tile, 1), F32),) * HEADS_PER_VREG, masked=True)
    store_logits(i - 1, zb_ref)

    def pair(pp, st):
        carries, acc = st
        n = 2 * pp + 1
        store_logits(i - n - 1, za_ref)
        acc = acc + weighted_values(wa_ref, i - n + 1)
        carries = store_weights(zb_ref, wb_ref, carries)
        store_logits(i - n - 2, zb_ref)
        acc = acc + weighted_values(wb_ref, i - n)
        carries = store_weights(za_ref, wa_ref, carries, live=i - n - 1 >= 0)
        return carries, acc

    _, acc = lax.fori_loop(0, (i + 1) // 2, pair, (carries, jnp.zeros((tile, LANES), F32)))
    acc = acc + weighted_values(wa_ref, 0)
    o_ref[...] = acc.astype(o_ref.dtype)


def _attention(proj3, attn_width):
    b, s, _ = proj3.shape
    tile = min(ATTN_TILE, s)
    n_pairs = attn_width // LANES
    return pl.pallas_call(
        functools.partial(_attn_kernel, tile=tile),
        grid=(b, n_pairs, s // tile),
        in_specs=[
            pl.BlockSpec((None, tile, LANES), lambda bi, p, i: (bi, i, p)),
            pl.BlockSpec((None, s, LANES), lambda bi, p, i: (bi, 0, n_pairs + p)),
            pl.BlockSpec((None, s, LANES), lambda bi, p, i: (bi, 0, 2 * n_pairs + p)),
        ],
        out_specs=pl.BlockSpec((None, tile, LANES), lambda bi, p, i: (bi, i, p)),
        out_shape=jax.ShapeDtypeStruct((b, s, attn_width), BF16),
        scratch_shapes=[
            pltpu.VMEM((HEADS_PER_VREG, tile, tile), F32), pltpu.VMEM((HEADS_PER_VREG, tile, tile), F32),
            pltpu.VMEM((tile, HEADS_PER_VREG * tile), BF16), pltpu.VMEM((tile, HEADS_PER_VREG * tile), BF16),
        ],
        compiler_params=_params(("parallel", "parallel", "arbitrary")),
        name="sb_attention",
    )(proj3, proj3, proj3)


def _ssd_kernel(xbc_ref, z_ref, dtc_ref, dtr_ref, cw_ref, cb_ref, dtb_c_ref, dtb_r_ref, alog_c_ref,
                alog_r_ref, dskip_ref, gn_ref, y_ref, ext_ref, state_ref, *, chunk, width, heads):
    c = pl.program_id(1)
    heads_per_group = heads // SSD_GROUPS
    group_width = heads_per_group * HEAD_DIM
    gs = SSD_GROUPS * SSD_STATE

    @pl.when(c == 0)
    def _():
        ext_ref[0:CONV_HALO, :] = jnp.zeros((CONV_HALO, ext_ref.shape[1]), F32)
        state_ref[...] = jnp.zeros(state_ref.shape, F32)

    ext_ref[CONV_HALO:CONV_HALO + chunk, :] = xbc_ref[...].astype(F32)
    conv = cb_ref[...]
    for k in range(CONV_K):
        conv = conv + cw_ref[k:k + 1, :] * ext_ref[pl.ds(CONV_HALO - (CONV_K - 1) + k, chunk), :]
    ext_ref[0:CONV_HALO, :] = ext_ref[chunk:chunk + CONV_HALO, :]
    xbc = _silu(conv)
    xs = xbc[:, :width]
    bm = xbc[:, width:width + gs]
    cm = xbc[:, width + gs:]

    dt_c = _softplus(dtc_ref[...] + dtb_c_ref[...])
    dt_r = _softplus(dtr_ref[...] + dtb_r_ref[...])
    da_c = dt_c * (-jnp.exp(alog_c_ref[...]))
    da_r = dt_r * (-jnp.exp(alog_r_ref[...]))
    row = lax.broadcasted_iota(jnp.int32, (chunk, chunk), 0)
    col = lax.broadcasted_iota(jnp.int32, (chunk, chunk), 1)
    lower = row >= col
    tri_l = jnp.where(lower, 1.0, 0.0)
    tri_u = jnp.where(row <= col, 1.0, 0.0)
    cum_c = jnp.dot(tri_l, da_c, precision=HIGHEST, preferred_element_type=F32)
    cum_r = jnp.dot(da_r, tri_u, precision=HIGHEST, preferred_element_type=F32)
    last_c = cum_c[chunk - 1:chunk, :]
    last_r = cum_r[:, chunk - 1:chunk]
    tail_r = jnp.exp(last_r - cum_r)

    lane = lax.broadcasted_iota(jnp.int32, (chunk, LANES), 1)

    def per_head_lanes(a):
        first = lane[:a.shape[0]] < HEAD_DIM
        return jnp.concatenate([jnp.where(first, a[:, p:p + 1], a[:, p + 1:p + 2])
                                for p in range(0, heads, HEADS_PER_VREG)], axis=1)

    dt_x = per_head_lanes(dt_c)
    head_x = per_head_lanes(jnp.exp(cum_c))
    last_x = per_head_lanes(jnp.exp(last_c))
    xdt = xs * dt_x

    y_parts = []
    new_state = []
    for g in range(SSD_GROUPS):
        b_g = bm[:, g * SSD_STATE:(g + 1) * SSD_STATE]
        c_g = cm[:, g * SSD_STATE:(g + 1) * SSD_STATE]
        cb = _dot_nt(c_g.astype(BF16), b_g.astype(BF16))
        bt_g = b_g.T
        st_g = state_ref[:, g * group_width:(g + 1) * group_width]
        y_off = _dot(c_g.astype(BF16), st_g.astype(BF16))
        for p in range(group_width // LANES):
            lo = g * group_width + p * LANES
            xdt_p = xdt[:, lo:lo + LANES]
            y_p = jnp.zeros((chunk, LANES), F32)
            s_p = jnp.zeros((SSD_STATE, LANES), F32)
            for hh in range(HEADS_PER_VREG):
                h = lo // HEAD_DIM + hh
                head_lanes = (lane >= hh * HEAD_DIM) & (lane < (hh + 1) * HEAD_DIM)
                x_h = jnp.where(head_lanes, xdt_p, 0.0).astype(BF16)
                seg = cum_c[:, h:h + 1] - cum_r[h:h + 1, :]
                decay = jnp.exp(jnp.where(lower, seg, NEG_BIG))
                y_p = y_p + _dot((cb * decay).astype(BF16), x_h)
                s_p = s_p + _dot((bt_g * tail_r[h:h + 1, :]).astype(BF16), x_h)
            y_parts.append(y_p + y_off[:, p * LANES:(p + 1) * LANES] * head_x[:, lo:lo + LANES])
            new_state.append(s_p)
    y = jnp.concatenate(y_parts, axis=1) + dskip_ref[...] * xs
    state_ref[...] = state_ref[...] * last_x + jnp.concatenate(new_state, axis=1)

    y = y * _silu(z_ref[...].astype(F32))
    normed = []
    for g in range(SSD_GROUPS):
        yg = y[:, g * group_width:(g + 1) * group_width]
        normed.append(yg * lax.rsqrt(jnp.mean(yg * yg, axis=-1, keepdims=True) + EPS))
    y_ref[...] = (jnp.concatenate(normed, axis=1) * gn_ref[...]).astype(y_ref.dtype)


def _ssd(proj3, dtc3, dtr, conv_w, conv_b, dt_bias, a_log, d_skip, ssd_norm_g, attn_width):
    b, s, _ = proj3.shape
    heads = dt_bias.shape[0]
    width = heads * HEAD_DIM
    conv_dim = conv_w.shape[1]
    chunk = min(SSD_CHUNK, s)
    nc = s // chunk
    z_blk = 3 * attn_width // width
    xbc_blk = (3 * attn_width + width) // conv_dim
    small = lambda shape: pl.BlockSpec(shape, lambda bi, ci: (0, 0))
    return pl.pallas_call(
        functools.partial(_ssd_kernel, chunk=chunk, width=width, heads=heads),
        grid=(b, nc),
        in_specs=[
            pl.BlockSpec((None, chunk, conv_dim), lambda bi, ci: (bi, ci, xbc_blk)),
            pl.BlockSpec((None, chunk, width), lambda bi, ci: (bi, ci, z_blk)),
            pl.BlockSpec((None, chunk, heads), lambda bi, ci: (bi, ci, 0)),
            pl.BlockSpec((heads, chunk), lambda bi, ci: (0, bi * nc + ci)),
            small((CONV_K, conv_dim)), small((1, conv_dim)),
            small((1, heads)), small((heads, 1)), small((1, heads)), small((heads, 1)),
            small((1, width)), small((1, width)),
        ],
        out_specs=pl.BlockSpec((None, chunk, width), lambda bi, ci: (bi, ci, 0)),
        out_shape=jax.ShapeDtypeStruct((b, s, width), BF16),
        scratch_shapes=[
            pltpu.VMEM((chunk + CONV_HALO, conv_dim), F32),
            pltpu.VMEM((SSD_STATE, width), F32),
        ],
        compiler_params=_params(("parallel", "arbitrary")),
        name="conv_ssd",
    )(proj3, proj3, dtc3, dtr, conv_w, conv_b.reshape(1, conv_dim),
      dt_bias.reshape(1, heads), dt_bias.reshape(heads, 1), a_log.reshape(1, heads), a_log.reshape(heads, 1),
      jnp.repeat(d_skip, HEAD_DIM).reshape(1, width), ssd_norm_g.reshape(1, width))


def _outproj_kernel(attn_ref, y_ref, x_ref, ga_ref, wa_ref, wy_ref, gf_ref, x1_ref, h_ref):
    a = _rms(attn_ref[...].astype(F32), ga_ref[...]).astype(BF16)
    x1 = x_ref[...] + _dot(a, wa_ref[...]) + _dot(y_ref[...], wy_ref[...])
    x1_ref[...] = x1
    h_ref[...] = _rms(x1, gf_ref[...]).astype(h_ref.dtype)


def _outproj(attn2, y2, x2, attn_norm_g, w_out, ffn_norm_g):
    t, d = x2.shape
    wa = attn2.shape[1]
    wy = y2.shape[1]
    tm = min(ROW_TILE, t)
    w_bf = w_out.astype(BF16)
    return pl.pallas_call(
        _outproj_kernel,
        grid=(t // tm,),
        in_specs=[
            pl.BlockSpec((tm, wa), lambda i: (i, 0)),
            pl.BlockSpec((tm, wy), lambda i: (i, 0)),
            pl.BlockSpec((tm, d), lambda i: (i, 0)),
            pl.BlockSpec((1, wa), lambda i: (0, 0)),
            pl.BlockSpec((wa, d), lambda i: (0, 0)),
            pl.BlockSpec((wy, d), lambda i: (0, 0)),
            pl.BlockSpec((1, d), lambda i: (0, 0)),
        ],
        out_specs=[pl.BlockSpec((tm, d), lambda i: (i, 0)), pl.BlockSpec((tm, d), lambda i: (i, 0))],
        out_shape=[jax.ShapeDtypeStruct((t, d), F32), jax.ShapeDtypeStruct((t, d), BF16)],
        compiler_params=_params(("parallel",)),
        name="outproj",
    )(attn2, y2, x2, attn_norm_g.reshape(1, wa), w_bf[:wa], w_bf[wa:], ffn_norm_g.reshape(1, d))


def _swiglu(x, wg_ref, wu_ref, wd_ref):
    fc = wg_ref.shape[-1]
    sub = _ff_chunk(fc, FF_SUB)
    part = None
    for c in range(0, fc, sub):
        act = (_silu(_dot(x, wg_ref[:, c:c + sub])) * _dot(x, wu_ref[:, c:c + sub])).astype(BF16)
        piece = _dot(act, wd_ref[c:c + sub, :])
        part = piece if part is None else part + piece
    return part


def _dense_ffn_kernel(h_ref, x_ref, wg_ref, wu_ref, wd_ref, gfin_ref, o_ref, acc_ref, *, final_norm):
    j = pl.program_id(1)
    part = _swiglu(h_ref[...], wg_ref, wu_ref, wd_ref)

    @pl.when(j == 0)
    def _():
        acc_ref[...] = x_ref[...] + part

    @pl.when(j > 0)
    def _():
        acc_ref[...] += part

    @pl.when(j == pl.num_programs(1) - 1)
    def _():
        out = acc_ref[...]
        o_ref[...] = _rms(out, gfin_ref[...]) if final_norm else out


def _ff_chunk(d_ff, target):
    best = LANES
    for c in range(LANES, min(d_ff, target) + 1, LANES):
        if d_ff % c == 0:
            best = c
    return best


def _dense_ffn(h2, x1, wg, wu, wd, final_g, final_norm):
    t, d = x1.shape
    d_ff = wg.shape[1]
    tm = min(ROW_TILE, t)
    fc = d_ff
    once = pl.Buffered(1)
    return pl.pallas_call(
        functools.partial(_dense_ffn_kernel, final_norm=final_norm),
        grid=(t // tm, d_ff // fc),
        in_specs=[
            pl.BlockSpec((tm, d), lambda i, j: (i, 0)),
            pl.BlockSpec((tm, d), lambda i, j: (i, 0)),
            pl.BlockSpec((d, fc), lambda i, j: (0, j), pipeline_mode=once),
            pl.BlockSpec((d, fc), lambda i, j: (0, j), pipeline_mode=once),
            pl.BlockSpec((fc, d), lambda i, j: (j, 0), pipeline_mode=once),
            pl.BlockSpec((1, d), lambda i, j: (0, 0)),
        ],
        out_specs=pl.BlockSpec((tm, d), lambda i, j: (i, 0)),
        out_shape=jax.ShapeDtypeStruct((t, d), F32),
        scratch_shapes=[pltpu.VMEM((tm, d), F32)],
        compiler_params=_params(("parallel", "arbitrary")),
        name="dense_ffn",
    )(h2, x1, wg.astype(BF16), wu.astype(BF16), wd.astype(BF16), final_g.reshape(1, d))


def _router_kernel(x_ref, g_ref, rwt_ref, ri_ref, rg_ref, cum_ref, carry_ref, *, n_exp):
    blk = x_ref.shape[0]

    @pl.when(pl.program_id(0) == 0)
    def _():
        carry_ref[...] = jnp.zeros(carry_ref.shape, F32)

    h = _rms(x_ref[...], g_ref[...])
    logits = _dot_nt(rwt_ref[...], h, precision=HIGHEST)
    eidx = lax.broadcasted_iota(jnp.int32, (n_exp, blk), 0)
    m1 = jnp.max(logits, axis=0, keepdims=True)
    i1 = jnp.min(jnp.where(logits == m1, eidx, n_exp), axis=0, keepdims=True)
    rest = jnp.where(eidx == i1, -jnp.inf, logits)
    m2 = jnp.max(rest, axis=0, keepdims=True)
    i2 = jnp.min(jnp.where(rest == m2, eidx, n_exp), axis=0, keepdims=True)
    e21 = jnp.exp(m2 - m1)
    g1 = 1.0 / (1.0 + e21)
    g2 = e21 * g1

    sel1 = eidx == i1
    sel2 = eidx == i2
    onehot = jnp.where(sel1 | sel2, 1.0, 0.0)
    row = lax.broadcasted_iota(jnp.int32, (blk, blk), 0)
    col = lax.broadcasted_iota(jnp.int32, (blk, blk), 1)
    tri_u = jnp.where(row <= col, 1.0, 0.0).astype(BF16)
    incl = _dot(onehot.astype(BF16), tri_u)
    carry = carry_ref[:, 0:1]
    rank = carry + incl - onehot
    r1 = jnp.sum(jnp.where(sel1, rank, 0.0), axis=0, keepdims=True)
    r2 = jnp.sum(jnp.where(sel2, rank, 0.0), axis=0, keepdims=True)
    total = carry + jnp.sum(onehot, axis=1, keepdims=True)
    carry_ref[...] = jnp.broadcast_to(total, carry_ref.shape)
    cum_ref[...] = jnp.broadcast_to(total, cum_ref.shape).astype(jnp.int32)

    ri_ref[0:1, :] = i1
    ri_ref[1:2, :] = i2
    ri_ref[2:3, :] = r1.astype(jnp.int32)
    ri_ref[3:4, :] = r2.astype(jnp.int32)
    rg_ref[0:1, :] = g1
    rg_ref[1:2, :] = g2


def _router(x1, ffn_g, router_w):
    t, d = x1.shape
    n_exp = router_w.shape[1]
    blk = min(TOKEN_BLOCK, t)
    nb = t // blk
    return pl.pallas_call(
        functools.partial(_router_kernel, n_exp=n_exp),
        grid=(nb,),
        in_specs=[
            pl.BlockSpec((blk, d), lambda i: (i, 0)),
            pl.BlockSpec((1, d), lambda i: (0, 0)),
            pl.BlockSpec((n_exp, d), lambda i: (0, 0)),
        ],
        out_specs=[
            pl.BlockSpec((4, blk), lambda i: (0, i)),
            pl.BlockSpec((2, blk), lambda i: (0, i)),
            pl.BlockSpec((None, n_exp, LANES), lambda i: (i, 0, 0)),
        ],
        out_shape=[
            jax.ShapeDtypeStruct((4, t), jnp.int32),
            jax.ShapeDtypeStruct((2, t), F32),
            jax.ShapeDtypeStruct((nb, n_exp, LANES), jnp.int32),
        ],
        scratch_shapes=[pltpu.VMEM((n_exp, LANES), F32)],
        compiler_params=_params(("arbitrary",)),
        name="router",
    )(x1, ffn_g.reshape(1, d), router_w.T)


def _gather_kernel(b0_ref, b1_ref, pos_ref, gate_ref, h_ref, xs_ref, gs_ref, acc_ref, gacc_ref, *, tile, blk):
    i = pl.program_id(0)
    slot = i * tile + lax.broadcasted_iota(jnp.int32, (tile, 1), 0)
    acc_ref[...] = jnp.zeros(acc_ref.shape, F32)
    gacc_ref[...] = jnp.zeros(gacc_ref.shape, F32)

    def body(b, carry):
        pos = pos_ref[b]
        gate = gate_ref[b]
        rows = acc_ref[...]
        gsum = gacc_ref[...]
        piece = min(MXU_DEPTH, blk)
        for c in range(0, blk, piece):
            hit1 = pos[0:1, c:c + piece] == slot
            hit2 = pos[1:2, c:c + piece] == slot
            onehot = jnp.where(hit1 | hit2, 1.0, 0.0).astype(BF16)
            start = pl.multiple_of(b * blk + c, piece)
            rows = rows + _dot(onehot, h_ref[pl.ds(start, piece), :])
            gsel = jnp.where(hit1, gate[0:1, c:c + piece], 0.0) + jnp.where(hit2, gate[1:2, c:c + piece], 0.0)
            gsum = gsum + jnp.sum(gsel, axis=1, keepdims=True)
        acc_ref[...] = rows
        gacc_ref[...] = gsum
        return carry

    lax.fori_loop(b0_ref[i], b1_ref[i], body, 0)
    xs_ref[...] = acc_ref[...].astype(xs_ref.dtype)
    gs_ref[...] = gacc_ref[...]


def _gather(b0, b1, pos3, gate3, h2, n_tiles, tile):
    t, d = h2.shape
    nb, _, blk = pos3.shape
    grid_spec = pltpu.PrefetchScalarGridSpec(
        num_scalar_prefetch=2,
        grid=(n_tiles,),
        in_specs=[
            pl.BlockSpec((nb, 2, blk), lambda i, *_: (0, 0, 0)),
            pl.BlockSpec((nb, 2, blk), lambda i, *_: (0, 0, 0)),
            pl.BlockSpec((t, d), lambda i, *_: (0, 0), pipeline_mode=pl.Buffered(1)),
        ],
        out_specs=[
            pl.BlockSpec((tile, d), lambda i, *_: (i, 0)),
            pl.BlockSpec((tile, 1), lambda i, *_: (i, 0)),
        ],
        scratch_shapes=[pltpu.VMEM((tile, d), F32), pltpu.VMEM((tile, 1), F32)],
    )
    return pl.pallas_call(
        functools.partial(_gather_kernel, tile=tile, blk=blk),
        grid_spec=grid_spec,
        out_shape=[
            jax.ShapeDtypeStruct((n_tiles * tile, d), BF16),
            jax.ShapeDtypeStruct((n_tiles * tile, 1), F32),
        ],
        compiler_params=_params(("parallel",)),
        name="moe_gather",
    )(b0, b1, pos3, gate3, h2)


def _expert_kernel(te_ref, nv_ref, xs_ref, gs_ref, wg_ref, wu_ref, wd_ref, y_ref, acc_ref):
    i = pl.program_id(0)
    j = pl.program_id(1)
    valid = i < nv_ref[0]

    @pl.when((i == 0) & (j == 0))
    def _():
        acc_ref[...] = jnp.zeros(acc_ref.shape, F32)

    @pl.when(valid)
    def _():
        total = jnp.where(j == 0, 0.0, acc_ref[...]) + _swiglu(xs_ref[...], wg_ref, wu_ref, wd_ref)
        acc_ref[...] = total
        y_ref[...] = (total * gs_ref[...]).astype(y_ref.dtype)

    @pl.when(jnp.logical_not(valid))
    def _():
        y_ref[...] = jnp.zeros(y_ref.shape, y_ref.dtype)


def _experts(tile_expert, n_valid, xs, gs, wg, wu, wd, tile):
    p, d = xs.shape
    n_tiles = p // tile
    d_ff = wg.shape[2]
    fc = _ff_chunk(d_ff, 1792)
    nj = d_ff // fc

    def chunk(i, j, nv):
        return jnp.where(i < nv[0], j, nj - 1)

    grid_spec = pltpu.PrefetchScalarGridSpec(
        num_scalar_prefetch=2,
        grid=(n_tiles, nj),
        in_specs=[
            pl.BlockSpec((tile, d), lambda i, j, te, nv: (i, 0)),
            pl.BlockSpec((tile, 1), lambda i, j, te, nv: (i, 0)),
            pl.BlockSpec((None, d, fc), lambda i, j, te, nv: (te[i], 0, chunk(i, j, nv))),
            pl.BlockSpec((None, d, fc), lambda i, j, te, nv: (te[i], 0, chunk(i, j, nv))),
            pl.BlockSpec((None, fc, d), lambda i, j, te, nv: (te[i], chunk(i, j, nv), 0)),
        ],
        out_specs=pl.BlockSpec((tile, d), lambda i, j, te, nv: (i, 0)),
        scratch_shapes=[pltpu.VMEM((tile, d), F32)],
    )
    return pl.pallas_call(
        _expert_kernel,
        grid_spec=grid_spec,
        out_shape=jax.ShapeDtypeStruct((p, d), BF16),
        compiler_params=_params(("parallel", "arbitrary")),
        name="moe_experts",
    )(tile_expert, n_valid, xs, gs, wg.astype(BF16), wu.astype(BF16), wd.astype(BF16))


def _combine_kernel(ib_ref, it_ref, fl_ref, ys_ref, pos_ref, x_ref, gfin_ref, o_ref, acc_ref, *, tile, final_norm):
    w = pl.program_id(0)
    flags = fl_ref[w]

    @pl.when(w == 0)
    def _():
        acc_ref[...] = jnp.zeros(acc_ref.shape, F32)

    first = (flags & 1) != 0
    live = (flags & 4) != 0
    pos = pos_ref[...]
    total = jnp.where(first, x_ref[...], acc_ref[...])
    piece = min(MXU_DEPTH, tile)
    for c in range(0, tile, piece):
        slot = it_ref[w] * tile + c + lax.broadcasted_iota(jnp.int32, (1, piece), 1)
        hit = ((pos[:, 0:1] == slot) | (pos[:, 1:2] == slot)) & live
        total = total + _dot(jnp.where(hit, 1.0, 0.0).astype(BF16), ys_ref[c:c + piece, :])
    acc_ref[...] = total

    @pl.when((flags & 2) != 0)
    def _():
        out = acc_ref[...]
        o_ref[...] = _rms(out, gfin_ref[...]) if final_norm else out


def _combine(item_block, item_tile, item_flags, ys, pos_col, x1, final_g, tile, final_norm):
    t, d = x1.shape
    blk = min(TOKEN_BLOCK, t)
    n_items = item_block.shape[0]
    grid_spec = pltpu.PrefetchScalarGridSpec(
        num_scalar_prefetch=3,
        grid=(n_items,),
        in_specs=[
            pl.BlockSpec((tile, d), lambda w, ib, it, fl: (it[w], 0)),
            pl.BlockSpec((blk, 2), lambda w, ib, it, fl: (ib[w], 0)),
            pl.BlockSpec((blk, d), lambda w, ib, it, fl: (ib[w], 0)),
            pl.BlockSpec((1, d), lambda w, ib, it, fl: (0, 0)),
        ],
        out_specs=pl.BlockSpec((blk, d), lambda w, ib, it, fl: (ib[w], 0)),
        scratch_shapes=[pltpu.VMEM((blk, d), F32)],
    )
    return pl.pallas_call(
        functools.partial(_combine_kernel, tile=tile, final_norm=final_norm),
        grid_spec=grid_spec,
        out_shape=jax.ShapeDtypeStruct((t, d), F32),
        compiler_params=_params(("arbitrary",)),
        name="moe_combine",
    )(item_block, item_tile, item_flags, ys, pos_col, x1, final_g.reshape(1, d))


def _moe(x1, h2, ffn_g, router_w, wg, wu, wd, final_g, final_norm):
    t, d = x1.shape
    n_exp = router_w.shape[1]
    blk = min(TOKEN_BLOCK, t)
    nb = t // blk
    tile = min(EXPERT_TILE, t)
    n_tiles = TOP_K * t // tile + n_exp
    i32 = jnp.int32

    ri, rg, cum_blocks = _router(x1, ffn_g, router_w)

    cum = jnp.concatenate([jnp.zeros((1, n_exp), i32), cum_blocks[:, :, 0]], axis=0)
    counts = cum[nb]
    tiles_per = (counts + tile - 1) // tile
    tile_end = jnp.cumsum(tiles_per)
    tile_start = tile_end - tiles_per
    seg_start = tile_start * tile
    n_valid = tile_end[n_exp - 1]
    tile_ids = jnp.arange(n_tiles, dtype=i32)
    tile_expert = jnp.minimum(jnp.sum(tile_ids[:, None] >= tile_end[None, :], axis=1), n_exp - 1).astype(i32)
    tile_expert = jnp.where(tile_ids < n_valid, tile_expert, tile_expert[jnp.maximum(n_valid - 1, 0)])

    seg_of = jnp.sum(jnp.where(ri[0:2, :, None] == jnp.arange(n_exp, dtype=i32), seg_start, 0), axis=-1)
    pos = seg_of + ri[2:4]
    pos3 = pos.reshape(2, nb, blk).transpose(1, 0, 2)
    gate3 = rg.reshape(2, nb, blk).transpose(1, 0, 2)

    r0 = (tile_ids - tile_start[tile_expert]) * tile
    cum_e = cum[:, tile_expert]
    b0 = jnp.sum(cum_e[1:] <= r0[None, :], axis=0)
    b1 = jnp.sum(cum_e[:-1] < (r0 + tile)[None, :], axis=0)
    live = tile_ids < n_valid
    b0 = jnp.where(live, b0, 0).astype(i32)
    b1 = jnp.where(live, b1, 0).astype(i32)

    xs, gs = _gather(b0, b1, pos3, gate3, h2, n_tiles, tile)
    ys = _experts(tile_expert, n_valid.reshape(1).astype(i32), xs, gs, wg, wu, wd, tile)

    lo = seg_start[None, :] + cum[:-1]
    hi = seg_start[None, :] + cum[1:]
    first_tile = lo // tile
    n_pair = jnp.where(hi > lo, (hi - 1) // tile - first_tile + 1, 0).reshape(-1)
    pair_end = jnp.cumsum(n_pair)
    pair_start = pair_end - n_pair
    n_items = nb * n_exp + n_tiles
    total = pair_end[-1]
    w_ids = jnp.arange(n_items, dtype=i32)
    w_clamped = jnp.minimum(w_ids, jnp.maximum(total - 1, 0))
    pair = jnp.minimum(jnp.sum(w_clamped[:, None] >= pair_end[None, :], axis=1), nb * n_exp - 1)
    item_block = (pair // n_exp).astype(i32)
    item_tile = (first_tile.reshape(-1)[pair] + w_clamped - pair_start[pair]).astype(i32)
    valid = w_ids < total
    prev_block = jnp.concatenate([jnp.full((1,), -1, i32), item_block[:-1]])
    next_block = jnp.concatenate([item_block[1:], jnp.full((1,), -1, i32)])
    is_first = valid & (item_block != prev_block)
    is_last = valid & ((item_block != next_block) | (w_ids == total - 1))
    item_flags = (is_first * 1 + is_last * 2 + valid * 4).astype(i32)

    return _combine(item_block, item_tile, item_flags, ys, pos.T, x1, final_g, tile, final_norm)


def kernel(x, mix_norm_g, w_in, conv_w, conv_b, dt_bias, a_log, d_skip, attn_norm_g, ssd_norm_g, w_out, ffn_norm_g, dense_w_gate, dense_w_up, dense_w_down, router_w, moe_w_gate, moe_w_up, moe_w_down, final_norm_g):
    b, s, d = x.shape
    depth = w_in.shape[0]
    attn_width = attn_norm_g.shape[1]
    heads = dt_bias.shape[1]
    n_main = w_in.shape[2] - heads
    x2 = x.reshape(b * s, d)
    for layer in range(depth):
        proj, dtc, dtr = _inproj(x2, mix_norm_g[layer], w_in[layer, :, :n_main], w_in[layer, :, n_main:])
        proj3 = proj.reshape(b, s, n_main)
        attn = _attention(proj3, attn_width)
        y = _ssd(proj3, dtc.reshape(b, s, heads), dtr, conv_w[layer], conv_b[layer], dt_bias[layer],
                 a_log[layer], d_skip[layer], ssd_norm_g[layer], attn_width)
        x1, h2 = _outproj(attn.reshape(b * s, attn_width), y.reshape(b * s, -1), x2, attn_norm_g[layer],
                          w_out[layer], ffn_norm_g[layer])
        final = layer == depth - 1
        i = layer // 2
        if layer % 2 == 0:
            x2 = _dense_ffn(h2, x1, dense_w_gate[i], dense_w_up[i], dense_w_down[i], final_norm_g, final)
        else:
            x2 = _moe(x1, h2, ffn_norm_g[layer], router_w[i], moe_w_gate[i], moe_w_up[i], moe_w_down[i],
                      final_norm_g, final)
    return x2.reshape(b, s, d)
```

```python
import functools

import jax
import jax.numpy as jnp
from jax import lax
from jax.experimental import pallas as pl
from jax.experimental.pallas import tpu as pltpu

F32 = jnp.float32
BF16 = jnp.bfloat16
HIGHEST = lax.Precision.HIGHEST

HEAD_DIM = 64
HEADS_PER_VREG = 2
LANES = 128
SSD_GROUPS = 2
SSD_STATE = 128
CONV_K = 4
CONV_HALO = 8
TOP_K = 2
EPS = 1e-5
NEG_BIG = -1e30
LOG2_E = 1.4426950408889634
EXP2_CLAMP = 126.0

ROW_TILE = 512
FF_SUB = 256
ATTN_TILE = 256
SSD_CHUNK = 256
TOKEN_BLOCK = 512
EXPERT_TILE = 512
COMBINE_TILE = 256
VMEM_LIMIT = 56 * 1024 * 1024


def _params(semantics):
    return pltpu.CompilerParams(dimension_semantics=semantics, vmem_limit_bytes=VMEM_LIMIT)


def _rms(x, g):
    var = jnp.mean(x * x, axis=-1, keepdims=True)
    return x * lax.rsqrt(var + EPS) * g


def _silu(x):
    return x / (1.0 + jnp.exp(-x))


def _softplus(x):
    return jnp.maximum(x, 0.0) + jnp.log1p(jnp.exp(-jnp.abs(x)))


def _dot(a, b):
    return jnp.dot(a, b, preferred_element_type=F32)


def _dot_nt(a, b, precision=None):
    return lax.dot_general(a, b, (((1,), (1,)), ((), ())), precision=precision,
                           preferred_element_type=F32)


def _inproj_kernel(x_ref, g_ref, w_ref, proj_ref, dtc_ref, dtr_ref):
    h = _rms(x_ref[...], g_ref[...])
    res = _dot(h.astype(BF16), w_ref[...])
    n = proj_ref.shape[-1]
    n_dt = dtc_ref.shape[-1]
    proj_ref[...] = res[:, :n].astype(proj_ref.dtype)
    dt_raw = res[:, n:]
    dtc_ref[...] = dt_raw[:, :n_dt]
    dtr_ref[...] = dt_raw.T[:n_dt, :]


def _inproj(x2, g, w_main, w_dt):
    t, d = x2.shape
    n = w_main.shape[1]
    nh = w_dt.shape[1]
    w_all = jnp.concatenate([w_main, jnp.pad(w_dt, ((0, 0), (0, LANES - nh)))], axis=1).astype(BF16)
    tm = min(ROW_TILE, t)
    return pl.pallas_call(
        _inproj_kernel,
        grid=(t // tm,),
        in_specs=[
            pl.BlockSpec((tm, d), lambda i: (i, 0)),
            pl.BlockSpec((1, d), lambda i: (0, 0)),
            pl.BlockSpec((d, n + LANES), lambda i: (0, 0)),
        ],
        out_specs=[
            pl.BlockSpec((tm, n), lambda i: (i, 0)),
            pl.BlockSpec((tm, nh), lambda i: (i, 0)),
            pl.BlockSpec((nh, tm), lambda i: (0, i)),
        ],
        out_shape=[
            jax.ShapeDtypeStruct((t, n), BF16),
            jax.ShapeDtypeStruct((t, nh), F32),
            jax.ShapeDtypeStruct((nh, t), F32),
        ],
        compiler_params=_params(("parallel",)),
        name="inproj",
    )(x2, g.reshape(1, d), w_all)


def _attn_kernel(q_ref, k_ref, v_ref, o_ref, za_ref, zb_ref, wa_ref, wb_ref, bias_ref, *, tile, nq):
    i = pl.program_id(2)
    n_first = i + 1
    lane = lax.broadcasted_iota(jnp.int32, (tile, LANES), 1)
    row = lax.broadcasted_iota(jnp.int32, (tile, tile), 0)
    col = lax.broadcasted_iota(jnp.int32, (tile, tile), 1)
    m_incl = jnp.where(row >= col, 1.0, 0.0).astype(BF16)
    bias_ref[0] = jnp.zeros((tile, tile), F32)
    bias_ref[1] = jnp.where(col < row, 0.0, NEG_BIG)
    head_lanes = [(lane >= hh * HEAD_DIM) & (lane < (hh + 1) * HEAD_DIM) for hh in range(HEADS_PER_VREG)]

    def q_heads(t):
        q = q_ref[pl.ds(pl.multiple_of(t * tile, tile), tile), :].astype(F32) * (HEAD_DIM ** -0.5 * LOG2_E)
        return [jnp.where(m, q, 0.0).astype(BF16) for m in head_lanes]

    q_first, q_second = q_heads(i), q_heads(nq - 1 - i)

    def where_is(n):
        in_first = n < n_first
        t = jnp.where(in_first, i, nq - 1 - i)
        m = jnp.where(in_first, n, n - n_first)
        return in_first, t, t - m, m == 0

    def store_logits(n, z_ref):
        in_first, _, j, diag = where_is(n)
        ks = k_ref[pl.ds(pl.multiple_of(j * tile, tile), tile), :]
        bias = bias_ref[diag.astype(jnp.int32)]
        for hh in range(HEADS_PER_VREG):
            qh = jnp.where(in_first, q_first[hh], q_second[hh])
            z_ref[hh] = _dot_nt(qh, ks) + bias

    def store_weights(n, z_ref, w_ref, carries):
        diag = where_is(n)[3]
        new_carries = []
        for hh, carry in enumerate(carries):
            z2 = z_ref[hh]
            sp2 = jnp.maximum(jnp.log2(1.0 + jnp.exp2(jnp.minimum(z2, EXP2_CLAMP))), z2)
            cs2 = _dot(sp2.astype(BF16), m_incl) + jnp.where(diag, 0.0, carry)
            w = jnp.exp2(z_ref[hh] - cs2)
            w_ref[:, hh * tile:(hh + 1) * tile] = w.astype(BF16)
            new_carries.append(cs2[:, 0:1])
        return tuple(new_carries)

    def add_values(n, w_ref, acc):
        _, t, j, diag = where_is(n)
        vs = v_ref[pl.ds(pl.multiple_of(j * tile, tile), tile), :]
        v_heads = [jnp.where(m, vs, jnp.zeros_like(vs)) for m in head_lanes]
        acc = jnp.where(diag, 0.0, acc) + _dot(w_ref[...], jnp.concatenate(v_heads, axis=0))
        o_ref[pl.ds(pl.multiple_of(t * tile, tile), tile), :] = acc.astype(o_ref.dtype)
        return acc

    wb_ref[...] = jnp.zeros(wb_ref.shape, BF16)
    store_logits(0, za_ref)

    def pair(pp, st):
        carries, acc = st
        n = 2 * pp
        store_logits(n + 1, zb_ref)
        acc = add_values(jnp.maximum(n - 1, 0), wb_ref, acc)
        carries = store_weights(n, za_ref, wa_ref, carries)
        store_logits(n + 2, za_ref)
        acc = add_values(n, wa_ref, acc)
        carries = store_weights(n + 1, zb_ref, wb_ref, carries)
        return carries, acc

    zero = ((jnp.zeros((tile, 1), F32),) * HEADS_PER_VREG, jnp.zeros((tile, LANES), F32))
    carries, acc = lax.fori_loop(0, nq // 2, pair, zero)
    acc = add_values(nq - 1, wb_ref, acc)
    store_weights(nq, za_ref, wa_ref, carries)
    add_values(nq, wa_ref, acc)


def _attention(proj3, attn_width):
    b, s, _ = proj3.shape
    tile = min(ATTN_TILE, s // 2)
    nq = s // tile
    assert s % tile == 0 and nq % 2 == 0, "q tiles are processed in pairs (i, nq-1-i)"
    n_pairs = attn_width // LANES

    def whole_sequence(col):
        return pl.BlockSpec((None, s, LANES), lambda bi, p, i: (bi, 0, col * n_pairs + p))

    return pl.pallas_call(
        functools.partial(_attn_kernel, tile=tile, nq=nq),
        grid=(b, n_pairs, nq // 2),
        in_specs=[whole_sequence(0), whole_sequence(1), whole_sequence(2)],
        out_specs=whole_sequence(0),
        out_shape=jax.ShapeDtypeStruct((b, s, attn_width), BF16),
        scratch_shapes=[
            pltpu.VMEM((HEADS_PER_VREG, tile, tile), F32), pltpu.VMEM((HEADS_PER_VREG, tile, tile), F32),
            pltpu.VMEM((tile, HEADS_PER_VREG * tile), BF16), pltpu.VMEM((tile, HEADS_PER_VREG * tile), BF16),
            pltpu.VMEM((2, tile, tile), F32),
        ],
        compiler_params=_params(("parallel", "parallel", "arbitrary")),
        name="sb_attention",
    )(proj3, proj3, proj3)


def _ssd_kernel(xbc_ref, z_ref, dtc_ref, dtr_ref, cw_ref, cb_ref, dtb_c_ref, dtb_r_ref, alog_c_ref,
                alog_r_ref, dskip_ref, gn_ref, y_ref, ext_ref, state_ref, shift_ref, *, chunk, width, heads):
    c = pl.program_id(1)
    heads_per_group = heads // SSD_GROUPS
    group_width = heads_per_group * HEAD_DIM
    gs = SSD_GROUPS * SSD_STATE

    @pl.when(c == 0)
    def _():
        ext_ref[0:CONV_HALO, :] = jnp.zeros((CONV_HALO, ext_ref.shape[1]), F32)
        state_ref[...] = jnp.zeros(state_ref.shape, F32)
        row = lax.broadcasted_iota(jnp.int32, (chunk, chunk), 0)
        col = lax.broadcasted_iota(jnp.int32, (chunk, chunk), 1)
        for d in range(1, CONV_K):
            shift_ref[d - 1] = jnp.where(row - col == d, 1.0, 0.0).astype(BF16)

    u = xbc_ref[...]
    u32 = u.astype(F32)
    conv = cb_ref[...] + cw_ref[CONV_K - 1:CONV_K, :] * u32
    for d in range(1, CONV_K):
        conv = conv + cw_ref[CONV_K - 1 - d:CONV_K - d, :] * _dot(shift_ref[d - 1], u)
    ext_ref[CONV_HALO:2 * CONV_HALO, :] = u32[0:CONV_HALO]
    head = cb_ref[...]
    for k in range(CONV_K):
        head = head + cw_ref[k:k + 1, :] * ext_ref[pl.ds(CONV_HALO - (CONV_K - 1) + k, CONV_HALO), :]
    ext_ref[0:CONV_HALO, :] = u32[chunk - CONV_HALO:chunk]
    xbc = _silu(jnp.concatenate([head, conv[CONV_HALO:]], axis=0))
    xs = xbc[:, :width]
    bm = xbc[:, width:width + gs]
    cm = xbc[:, width + gs:]

    dt_c = _softplus(dtc_ref[...] + dtb_c_ref[...])
    dt_r = _softplus(dtr_ref[...] + dtb_r_ref[...])
    da_c = dt_c * (-jnp.exp(alog_c_ref[...]))
    da_r = dt_r * (-jnp.exp(alog_r_ref[...]))
    row = lax.broadcasted_iota(jnp.int32, (chunk, chunk), 0)
    col = lax.broadcasted_iota(jnp.int32, (chunk, chunk), 1)
    lower = row >= col
    tri_l = jnp.where(lower, 1.0, 0.0)
    tri_u = jnp.where(row <= col, 1.0, 0.0)
    cum_c = jnp.dot(tri_l, da_c, precision=HIGHEST, preferred_element_type=F32)
    cum_r = jnp.dot(da_r, tri_u, precision=HIGHEST, preferred_element_type=F32)
    last_c = cum_c[chunk - 1:chunk, :]
    last_r = cum_r[:, chunk - 1:chunk]
    tail_r = jnp.exp(last_r - cum_r)

    lane = lax.broadcasted_iota(jnp.int32, (chunk, LANES), 1)

    def per_head_lanes(a):
        first = lane[:a.shape[0]] < HEAD_DIM
        return jnp.concatenate([jnp.where(first, a[:, p:p + 1], a[:, p + 1:p + 2])
                                for p in range(0, heads, HEADS_PER_VREG)], axis=1)

    dt_x = per_head_lanes(dt_c)
    head_x = per_head_lanes(jnp.exp(cum_c))
    last_x = per_head_lanes(jnp.exp(last_c))
    xdt = xs * dt_x

    y_parts = []
    new_state = []
    for g in range(SSD_GROUPS):
        b_g = bm[:, g * SSD_STATE:(g + 1) * SSD_STATE]
        c_g = cm[:, g * SSD_STATE:(g + 1) * SSD_STATE]
        cb = _dot_nt(c_g.astype(BF16), b_g.astype(BF16))
        bt_g = b_g.T
        st_g = state_ref[:, g * group_width:(g + 1) * group_width]
        y_off = _dot(c_g.astype(BF16), st_g.astype(BF16))
        for p in range(group_width // LANES):
            lo = g * group_width + p * LANES
            xdt_p = xdt[:, lo:lo + LANES]
            y_p = jnp.zeros((chunk, LANES), F32)
            s_p = jnp.zeros((SSD_STATE, LANES), F32)
            for hh in range(HEADS_PER_VREG):
                h = lo // HEAD_DIM + hh
                head_lanes = (lane >= hh * HEAD_DIM) & (lane < (hh + 1) * HEAD_DIM)
                x_h = jnp.where(head_lanes, xdt_p, 0.0).astype(BF16)
                seg = cum_c[:, h:h + 1] - cum_r[h:h + 1, :]
                decay = jnp.exp(jnp.where(lower, seg, NEG_BIG))
                y_p = y_p + _dot((cb * decay).astype(BF16), x_h)
                s_p = s_p + _dot((bt_g * tail_r[h:h + 1, :]).astype(BF16), x_h)
            y_parts.append(y_p + y_off[:, p * LANES:(p + 1) * LANES] * head_x[:, lo:lo + LANES])
            new_state.append(s_p)
    y = jnp.concatenate(y_parts, axis=1) + dskip_ref[...] * xs
    state_ref[...] = state_ref[...] * last_x + jnp.concatenate(new_state, axis=1)

    y = y * _silu(z_ref[...].astype(F32))
    normed = []
    for g in range(SSD_GROUPS):
        yg = y[:, g * group_width:(g + 1) * group_width]
        normed.append(yg * lax.rsqrt(jnp.mean(yg * yg, axis=-1, keepdims=True) + EPS))
    y_ref[...] = (jnp.concatenate(normed, axis=1) * gn_ref[...]).astype(y_ref.dtype)


def _ssd(proj3, dtc3, dtr, conv_w, conv_b, dt_bias, a_log, d_skip, ssd_norm_g, attn_width):
    b, s, _ = proj3.shape
    heads = dt_bias.shape[0]
    width = heads * HEAD_DIM
    conv_dim = conv_w.shape[1]
    chunk = min(SSD_CHUNK, s)
    nc = s // chunk
    z_blk = 3 * attn_width // width
    xbc_blk = (3 * attn_width + width) // conv_dim
    small = lambda shape: pl.BlockSpec(shape, lambda bi, ci: (0, 0))
    return pl.pallas_call(
        functools.partial(_ssd_kernel, chunk=chunk, width=width, heads=heads),
        grid=(b, nc),
        in_specs=[
            pl.BlockSpec((None, chunk, conv_dim), lambda bi, ci: (bi, ci, xbc_blk)),
            pl.BlockSpec((None, chunk, width), lambda bi, ci: (bi, ci, z_blk)),
            pl.BlockSpec((None, chunk, heads), lambda bi, ci: (bi, ci, 0)),
            pl.BlockSpec((heads, chunk), lambda bi, ci: (0, bi * nc + ci)),
            small((CONV_K, conv_dim)), small((1, conv_dim)),
            small((1, heads)), small((heads, 1)), small((1, heads)), small((heads, 1)),
            small((1, width)), small((1, width)),
        ],
        out_specs=pl.BlockSpec((None, chunk, width), lambda bi, ci: (bi, ci, 0)),
        out_shape=jax.ShapeDtypeStruct((b, s, width), BF16),
        scratch_shapes=[
            pltpu.VMEM((2 * CONV_HALO, conv_dim), F32),
            pltpu.VMEM((SSD_STATE, width), F32),
            pltpu.VMEM((CONV_K - 1, chunk, chunk), BF16),
        ],
        compiler_params=_params(("parallel", "arbitrary")),
        name="conv_ssd",
    )(proj3, proj3, dtc3, dtr, conv_w, conv_b.reshape(1, conv_dim),
      dt_bias.reshape(1, heads), dt_bias.reshape(heads, 1), a_log.reshape(1, heads), a_log.reshape(heads, 1),
      jnp.repeat(d_skip, HEAD_DIM).reshape(1, width), ssd_norm_g.reshape(1, width))


def _outproj_kernel(attn_ref, y_ref, x_ref, ga_ref, wa_ref, wy_ref, gf_ref, x1_ref, h_ref):
    a = _rms(attn_ref[...].astype(F32), ga_ref[...]).astype(BF16)
    x1 = x_ref[...] + _dot(a, wa_ref[...]) + _dot(y_ref[...], wy_ref[...])
    x1_ref[...] = x1
    h_ref[...] = _rms(x1, gf_ref[...]).astype(h_ref.dtype)


def _outproj(attn2, y2, x2, attn_norm_g, w_out, ffn_norm_g):
    t, d = x2.shape
    wa = attn2.shape[1]
    wy = y2.shape[1]
    tm = min(ROW_TILE, t)
    w_bf = w_out.astype(BF16)
    return pl.pallas_call(
        _outproj_kernel,
        grid=(t // tm,),
        in_specs=[
            pl.BlockSpec((tm, wa), lambda i: (i, 0)),
            pl.BlockSpec((tm, wy), lambda i: (i, 0)),
            pl.BlockSpec((tm, d), lambda i: (i, 0)),
            pl.BlockSpec((1, wa), lambda i: (0, 0)),
            pl.BlockSpec((wa, d), lambda i: (0, 0)),
            pl.BlockSpec((wy, d), lambda i: (0, 0)),
            pl.BlockSpec((1, d), lambda i: (0, 0)),
        ],
        out_specs=[pl.BlockSpec((tm, d), lambda i: (i, 0)), pl.BlockSpec((tm, d), lambda i: (i, 0))],
        out_shape=[jax.ShapeDtypeStruct((t, d), F32), jax.ShapeDtypeStruct((t, d), BF16)],
        compiler_params=_params(("parallel",)),
        name="outproj",
    )(attn2, y2, x2, attn_norm_g.reshape(1, wa), w_bf[:wa], w_bf[wa:], ffn_norm_g.reshape(1, d))


def _swiglu(x, wg_ref, wu_ref, wd_ref):
    fc = wg_ref.shape[-1]
    sub = _ff_chunk(fc, FF_SUB)
    part = None
    for c in range(0, fc, sub):
        act = (_silu(_dot(x, wg_ref[:, c:c + sub])) * _dot(x, wu_ref[:, c:c + sub])).astype(BF16)
        piece = _dot(act, wd_ref[c:c + sub, :])
        part = piece if part is None else part + piece
    return part


def _dense_ffn_kernel(h_ref, x_ref, wg_ref, wu_ref, wd_ref, gfin_ref, o_ref, acc_ref, *, final_norm):
    j = pl.program_id(1)
    part = _swiglu(h_ref[...], wg_ref, wu_ref, wd_ref)

    @pl.when(j == 0)
    def _():
        acc_ref[...] = x_ref[...] + part

    @pl.when(j > 0)
    def _():
        acc_ref[...] += part

    @pl.when(j == pl.num_programs(1) - 1)
    def _():
        out = acc_ref[...]
        o_ref[...] = _rms(out, gfin_ref[...]) if final_norm else out


def _ff_chunk(d_ff, target):
    best = LANES
    for c in range(LANES, min(d_ff, target) + 1, LANES):
        if d_ff % c == 0:
            best = c
    return best


def _dense_ffn(h2, x1, wg, wu, wd, final_g, final_norm):
    t, d = x1.shape
    d_ff = wg.shape[1]
    tm = min(ROW_TILE, t)
    fc = d_ff
    once = pl.Buffered(1)
    return pl.pallas_call(
        functools.partial(_dense_ffn_kernel, final_norm=final_norm),
        grid=(t // tm, d_ff // fc),
        in_specs=[
            pl.BlockSpec((tm, d), lambda i, j: (i, 0)),
            pl.BlockSpec((tm, d), lambda i, j: (i, 0)),
            pl.BlockSpec((d, fc), lambda i, j: (0, j), pipeline_mode=once),
            pl.BlockSpec((d, fc), lambda i, j: (0, j), pipeline_mode=once),
            pl.BlockSpec((fc, d), lambda i, j: (j, 0), pipeline_mode=once),
            pl.BlockSpec((1, d), lambda i, j: (0, 0)),
        ],
        out_specs=pl.BlockSpec((tm, d), lambda i, j: (i, 0)),
        out_shape=jax.ShapeDtypeStruct((t, d), F32),
        scratch_shapes=[pltpu.VMEM((tm, d), F32)],
        compiler_params=_params(("parallel", "arbitrary")),
        name="dense_ffn",
    )(h2, x1, wg.astype(BF16), wu.astype(BF16), wd.astype(BF16), final_g.reshape(1, d))


def _router_kernel(x_ref, g_ref, rwt_ref, ri_ref, rg_ref, cum_ref, carry_ref, *, n_exp):
    blk = x_ref.shape[0]

    @pl.when(pl.program_id(0) == 0)
    def _():
        carry_ref[...] = jnp.zeros(carry_ref.shape, F32)

    h = _rms(x_ref[...], g_ref[...])
    logits = _dot_nt(rwt_ref[...], h, precision=HIGHEST)
    eidx = lax.broadcasted_iota(jnp.int32, (n_exp, blk), 0)
    m1 = jnp.max(logits, axis=0, keepdims=True)
    i1 = jnp.min(jnp.where(logits == m1, eidx, n_exp), axis=0, keepdims=True)
    rest = jnp.where(eidx == i1, -jnp.inf, logits)
    m2 = jnp.max(rest, axis=0, keepdims=True)
    i2 = jnp.min(jnp.where(rest == m2, eidx, n_exp), axis=0, keepdims=True)
    e21 = jnp.exp(m2 - m1)
    g1 = 1.0 / (1.0 + e21)
    g2 = e21 * g1

    sel1 = eidx == i1
    sel2 = eidx == i2
    onehot = jnp.where(sel1 | sel2, 1.0, 0.0)
    row = lax.broadcasted_iota(jnp.int32, (blk, blk), 0)
    col = lax.broadcasted_iota(jnp.int32, (blk, blk), 1)
    tri_u = jnp.where(row <= col, 1.0, 0.0).astype(BF16)
    incl = _dot(onehot.astype(BF16), tri_u)
    carry = carry_ref[:, 0:1]
    rank = carry + incl - onehot
    r1 = jnp.sum(jnp.where(sel1, rank, 0.0), axis=0, keepdims=True)
    r2 = jnp.sum(jnp.where(sel2, rank, 0.0), axis=0, keepdims=True)
    total = carry + jnp.sum(onehot, axis=1, keepdims=True)
    carry_ref[...] = jnp.broadcast_to(total, carry_ref.shape)
    cum_ref[...] = jnp.broadcast_to(total, cum_ref.shape).astype(jnp.int32)

    ri_ref[0:1, :] = i1
    ri_ref[1:2, :] = i2
    ri_ref[2:3, :] = r1.astype(jnp.int32)
    ri_ref[3:4, :] = r2.astype(jnp.int32)
    rg_ref[0:1, :] = g1
    rg_ref[1:2, :] = g2


def _router(x1, ffn_g, router_w):
    t, d = x1.shape
    n_exp = router_w.shape[1]
    blk = min(TOKEN_BLOCK, t)
    nb = t // blk
    return pl.pallas_call(
        functools.partial(_router_kernel, n_exp=n_exp),
        grid=(nb,),
        in_specs=[
            pl.BlockSpec((blk, d), lambda i: (i, 0)),
            pl.BlockSpec((1, d), lambda i: (0, 0)),
            pl.BlockSpec((n_exp, d), lambda i: (0, 0)),
        ],
        out_specs=[
            pl.BlockSpec((4, blk), lambda i: (0, i)),
            pl.BlockSpec((2, blk), lambda i: (0, i)),
            pl.BlockSpec((None, n_exp, LANES), lambda i: (i, 0, 0)),
        ],
        out_shape=[
            jax.ShapeDtypeStruct((4, t), jnp.int32),
            jax.ShapeDtypeStruct((2, t), F32),
            jax.ShapeDtypeStruct((nb, n_exp, LANES), jnp.int32),
        ],
        scratch_shapes=[pltpu.VMEM((n_exp, LANES), F32)],
        compiler_params=_params(("arbitrary",)),
        name="router",
    )(x1, ffn_g.reshape(1, d), router_w.T)


def _gather_kernel(b0_ref, b1_ref, pos_ref, gate_ref, h_ref, xs_ref, gs_ref, acc_ref, gacc_ref, *, tile, blk):
    i = pl.program_id(0)
    slot = i * tile + lax.broadcasted_iota(jnp.int32, (tile, 1), 0)
    acc_ref[...] = jnp.zeros(acc_ref.shape, F32)
    gacc_ref[...] = jnp.zeros(gacc_ref.shape, F32)

    def body(b, carry):
        pos = pos_ref[b]
        gate = gate_ref[b]
        hit1 = pos[0:1, :] == slot
        hit2 = pos[1:2, :] == slot
        onehot = jnp.where(hit1 | hit2, 1.0, 0.0).astype(BF16)
        start = pl.multiple_of(b * blk, blk)
        acc_ref[...] += _dot(onehot, h_ref[pl.ds(start, blk), :])
        gsel = jnp.where(hit1, gate[0:1, :], 0.0) + jnp.where(hit2, gate[1:2, :], 0.0)
        gacc_ref[...] += jnp.sum(gsel, axis=1, keepdims=True)
        return carry

    lax.fori_loop(b0_ref[i], b1_ref[i], body, 0)
    xs_ref[...] = acc_ref[...].astype(xs_ref.dtype)
    gs_ref[...] = gacc_ref[...]


def _gather(b0, b1, pos3, gate3, h2, n_tiles, tile):
    t, d = h2.shape
    nb, _, blk = pos3.shape
    grid_spec = pltpu.PrefetchScalarGridSpec(
        num_scalar_prefetch=2,
        grid=(n_tiles,),
        in_specs=[
            pl.BlockSpec((nb, 2, blk), lambda i, *_: (0, 0, 0)),
            pl.BlockSpec((nb, 2, blk), lambda i, *_: (0, 0, 0)),
            pl.BlockSpec((t, d), lambda i, *_: (0, 0), pipeline_mode=pl.Buffered(1)),
        ],
        out_specs=[
            pl.BlockSpec((tile, d), lambda i, *_: (i, 0)),
            pl.BlockSpec((tile, 1), lambda i, *_: (i, 0)),
        ],
        scratch_shapes=[pltpu.VMEM((tile, d), F32), pltpu.VMEM((tile, 1), F32)],
    )
    return pl.pallas_call(
        functools.partial(_gather_kernel, tile=tile, blk=blk),
        grid_spec=grid_spec,
        out_shape=[
            jax.ShapeDtypeStruct((n_tiles * tile, d), BF16),
            jax.ShapeDtypeStruct((n_tiles * tile, 1), F32),
        ],
        compiler_params=_params(("parallel",)),
        name="moe_gather",
    )(b0, b1, pos3, gate3, h2)


def _expert_kernel(te_ref, nv_ref, xs_ref, gs_ref, wg_ref, wu_ref, wd_ref, y_ref, acc_ref):
    i = pl.program_id(0)
    j = pl.program_id(1)
    valid = i < nv_ref[0]

    @pl.when((i == 0) & (j == 0))
    def _():
        acc_ref[...] = jnp.zeros(acc_ref.shape, F32)

    @pl.when(valid)
    def _():
        total = jnp.where(j == 0, 0.0, acc_ref[...]) + _swiglu(xs_ref[...], wg_ref, wu_ref, wd_ref)
        acc_ref[...] = total
        y_ref[...] = (total * gs_ref[...]).astype(y_ref.dtype)

    @pl.when(jnp.logical_not(valid))
    def _():
        y_ref[...] = jnp.zeros(y_ref.shape, y_ref.dtype)


def _experts(tile_expert, n_valid, xs, gs, wg, wu, wd, tile):
    p, d = xs.shape
    n_tiles = p // tile
    d_ff = wg.shape[2]
    fc = _ff_chunk(d_ff, 1792)
    nj = d_ff // fc

    def chunk(i, j, nv):
        return jnp.where(i < nv[0], j, nj - 1)

    grid_spec = pltpu.PrefetchScalarGridSpec(
        num_scalar_prefetch=2,
        grid=(n_tiles, nj),
        in_specs=[
            pl.BlockSpec((tile, d), lambda i, j, te, nv: (i, 0)),
            pl.BlockSpec((tile, 1), lambda i, j, te, nv: (i, 0)),
            pl.BlockSpec((None, d, fc), lambda i, j, te, nv: (te[i], 0, chunk(i, j, nv))),
            pl.BlockSpec((None, d, fc), lambda i, j, te, nv: (te[i], 0, chunk(i, j, nv))),
            pl.BlockSpec((None, fc, d), lambda i, j, te, nv: (te[i], chunk(i, j, nv), 0)),
        ],
        out_specs=pl.BlockSpec((tile, d), lambda i, j, te, nv: (i, 0)),
        scratch_shapes=[pltpu.VMEM((tile, d), F32)],
    )
    return pl.pallas_call(
        _expert_kernel,
        grid_spec=grid_spec,
        out_shape=jax.ShapeDtypeStruct((p, d), BF16),
        compiler_params=_params(("parallel", "arbitrary")),
        name="moe_experts",
    )(tile_expert, n_valid, xs, gs, wg.astype(BF16), wu.astype(BF16), wd.astype(BF16))


def _combine_kernel(ib_ref, it_ref, fl_ref, ys_ref, pos_ref, x_ref, gfin_ref, o_ref, acc_ref, *, tile, final_norm):
    w = pl.program_id(0)
    flags = fl_ref[w]

    @pl.when((flags & 1) != 0)
    def _():
        acc_ref[...] = x_ref[...]

    @pl.when((flags & 4) != 0)
    def _():
        slot = it_ref[w] * tile + lax.broadcasted_iota(jnp.int32, (1, tile), 1)
        pos = pos_ref[...]
        hit = (pos[:, 0:1] == slot) | (pos[:, 1:2] == slot)
        acc_ref[...] += _dot(jnp.where(hit, 1.0, 0.0).astype(BF16), ys_ref[...])

    @pl.when((flags & 2) != 0)
    def _():
        out = acc_ref[...]
        o_ref[...] = _rms(out, gfin_ref[...]) if final_norm else out


def _combine(item_block, item_tile, item_flags, ys, pos_col, x1, final_g, tile, final_norm):
    t, d = x1.shape
    blk = min(TOKEN_BLOCK, t)
    n_items = item_block.shape[0]
    grid_spec = pltpu.PrefetchScalarGridSpec(
        num_scalar_prefetch=3,
        grid=(n_items,),
        in_specs=[
            pl.BlockSpec((tile, d), lambda w, ib, it, fl: (it[w], 0)),
            pl.BlockSpec((blk, 2), lambda w, ib, it, fl: (ib[w], 0)),
            pl.BlockSpec((blk, d), lambda w, ib, it, fl: (ib[w], 0)),
            pl.BlockSpec((1, d), lambda w, ib, it, fl: (0, 0)),
        ],
        out_specs=pl.BlockSpec((blk, d), lambda w, ib, it, fl: (ib[w], 0)),
        scratch_shapes=[pltpu.VMEM((blk, d), F32)],
    )
    return pl.pallas_call(
        functools.partial(_combine_kernel, tile=tile, final_norm=final_norm),
        grid_spec=grid_spec,
        out_shape=jax.ShapeDtypeStruct((t, d), F32),
        compiler_params=_params(("arbitrary",)),
        name="moe_combine",
    )(item_block, item_tile, item_flags, ys, pos_col, x1, final_g.reshape(1, d))


def _moe(x1, h2, ffn_g, router_w, wg, wu, wd, final_g, final_norm):
    t, d = x1.shape
    n_exp = router_w.shape[1]
    blk = min(TOKEN_BLOCK, t)
    nb = t // blk
    tile = min(EXPERT_TILE, t)
    n_tiles = TOP_K * t // tile + n_exp
    i32 = jnp.int32

    ri, rg, cum_blocks = _router(x1, ffn_g, router_w)

    cum = jnp.concatenate([jnp.zeros((1, n_exp), i32), cum_blocks[:, :, 0]], axis=0)
    counts = cum[nb]
    tiles_per = (counts + tile - 1) // tile
    tile_end = jnp.cumsum(tiles_per)
    tile_start = tile_end - tiles_per
    seg_start = tile_start * tile
    n_valid = tile_end[n_exp - 1]
    tile_ids = jnp.arange(n_tiles, dtype=i32)
    tile_expert = jnp.minimum(jnp.sum(tile_ids[:, None] >= tile_end[None, :], axis=1), n_exp - 1).astype(i32)
    tile_expert = jnp.where(tile_ids < n_valid, tile_expert, tile_expert[jnp.maximum(n_valid - 1, 0)])

    seg_of = jnp.sum(jnp.where(ri[0:2, :, None] == jnp.arange(n_exp, dtype=i32), seg_start, 0), axis=-1)
    pos = seg_of + ri[2:4]
    pos3 = pos.reshape(2, nb, blk).transpose(1, 0, 2)
    gate3 = rg.reshape(2, nb, blk).transpose(1, 0, 2)

    r0 = (tile_ids - tile_start[tile_expert]) * tile
    cum_e = cum[:, tile_expert]
    b0 = jnp.sum(cum_e[1:] <= r0[None, :], axis=0)
    b1 = jnp.sum(cum_e[:-1] < (r0 + tile)[None, :], axis=0)
    live = tile_ids < n_valid
    b0 = jnp.where(live, b0, 0).astype(i32)
    b1 = jnp.where(live, b1, 0).astype(i32)

    xs, gs = _gather(b0, b1, pos3, gate3, h2, n_tiles, tile)
    ys = _experts(tile_expert, n_valid.reshape(1).astype(i32), xs, gs, wg, wu, wd, tile)

    lo = seg_start[None, :] + cum[:-1]
    hi = seg_start[None, :] + cum[1:]
    ctile = min(COMBINE_TILE, tile)
    first_tile = lo // ctile
    n_pair = jnp.where(hi > lo, (hi - 1) // ctile - first_tile + 1, 0).reshape(-1)
    pair_end = jnp.cumsum(n_pair)
    pair_start = pair_end - n_pair
    n_items = nb * n_exp + n_tiles * (tile // ctile)
    total = pair_end[-1]
    w_ids = jnp.arange(n_items, dtype=i32)
    w_clamped = jnp.minimum(w_ids, jnp.maximum(total - 1, 0))
    pair = jnp.minimum(jnp.sum(w_clamped[:, None] >= pair_end[None, :], axis=1), nb * n_exp - 1)
    item_block = (pair // n_exp).astype(i32)
    item_tile = (first_tile.reshape(-1)[pair] + w_clamped - pair_start[pair]).astype(i32)
    valid = w_ids < total
    prev_block = jnp.concatenate([jnp.full((1,), -1, i32), item_block[:-1]])
    next_block = jnp.concatenate([item_block[1:], jnp.full((1,), -1, i32)])
    is_first = valid & (item_block != prev_block)
    is_last = valid & ((item_block != next_block) | (w_ids == total - 1))
    item_flags = (is_first * 1 + is_last * 2 + valid * 4).astype(i32)

    return _combine(item_block, item_tile, item_flags, ys, pos.T, x1, final_g, ctile, final_norm)


def kernel(x, mix_norm_g, w_in, conv_w, conv_b, dt_bias, a_log, d_skip, attn_norm_g, ssd_norm_g, w_out, ffn_norm_g, dense_w_gate, dense_w_up, dense_w_down, router_w, moe_w_gate, moe_w_up, moe_w_down, final_norm_g):
    b, s, d = x.shape
    depth = w_in.shape[0]
    attn_width = attn_norm_g.shape[1]
    heads = dt_bias.shape[1]
    n_main = w_in.shape[2] - heads
    x2 = x.reshape(b * s, d)
    for layer in range(depth):
        proj, dtc, dtr = _inproj(x2, mix_norm_g[layer], w_in[layer, :, :n_main], w_in[layer, :, n_main:])
        proj3 = proj.reshape(b, s, n_main)
        attn = _attention(proj3, attn_width)
        y = _ssd(proj3, dtc.reshape(b, s, heads), dtr, conv_w[layer], conv_b[layer], dt_bias[layer],
                 a_log[layer], d_skip[layer], ssd_norm_g[layer], attn_width)
        x1, h2 = _outproj(attn.reshape(b * s, attn_width), y.reshape(b * s, -1), x2, attn_norm_g[layer],
                          w_out[layer], ffn_norm_g[layer])
        final = layer == depth - 1
        i = layer // 2
        if layer % 2 == 0:
            x2 = _dense_ffn(h2, x1, dense_w_gate[i], dense_w_up[i], dense_w_down[i], final_norm_g, final)
        else:
            x2 = _moe(x1, h2, ffn_norm_g[layer], router_w[i], moe_w_gate[i], moe_w_up[i], moe_w_down[i],
                      final_norm_g, final)
    return x2.reshape(b, s, d)
```

```python
import functools

import jax
import jax.numpy as jnp
from jax import lax
from jax.experimental import pallas as pl
from jax.experimental.pallas import tpu as pltpu

F32 = jnp.float32
BF16 = jnp.bfloat16

HEAD_DIM = 64
HEADS_PER_VREG = 2
LANES = 128
SSD_GROUPS = 2
SSD_STATE = 128
CONV_K = 4
CONV_HALO = 8
TOP_K = 2
EPS = 1e-5
NEG_BIG = -1e30
LOG2_E = 1.4426950408889634
EXP2_CLAMP = 126.0

ROW_TILE = 512
FF_SUB = 256
ATTN_TILE = 256
SSD_CHUNK = 256
TOKEN_BLOCK = 512
EXPERT_TILE = 512
COMBINE_TILE = 256
VMEM_LIMIT = 56 * 1024 * 1024


def _params(semantics):
    return pltpu.CompilerParams(dimension_semantics=semantics, vmem_limit_bytes=VMEM_LIMIT)


def _rms(x, g):
    var = jnp.mean(x * x, axis=-1, keepdims=True)
    return x * lax.rsqrt(var + EPS) * g


def _silu(x):
    return x / (1.0 + jnp.exp(-x))


def _softplus(x):
    return jnp.maximum(x, 0.0) + jnp.log1p(jnp.exp(-jnp.abs(x)))


def _dot(a, b):
    return jnp.dot(a, b, preferred_element_type=F32)


def _dot_nt(a, b):
    return lax.dot_general(a, b, (((1,), (1,)), ((), ())), preferred_element_type=F32)


def _bf16_pieces(x, n):
    pieces = []
    for _ in range(n):
        piece = x.astype(BF16)
        pieces.append(piece)
        x = x - piece.astype(F32)
    return pieces


def _inproj_kernel(x_ref, g_ref, w_ref, proj_ref, dtc_ref, dtr_ref):
    h = _rms(x_ref[...], g_ref[...])
    res = _dot(h.astype(BF16), w_ref[...])
    n = proj_ref.shape[-1]
    n_dt = dtc_ref.shape[-1]
    proj_ref[...] = res[:, :n].astype(proj_ref.dtype)
    dt_raw = res[:, n:]
    dtc_ref[...] = dt_raw[:, :n_dt]
    dtr_ref[...] = dt_raw.T[:n_dt, :]


def _inproj(x2, g, w_main, w_dt):
    t, d = x2.shape
    n = w_main.shape[1]
    nh = w_dt.shape[1]
    w_all = jnp.concatenate([w_main, jnp.pad(w_dt, ((0, 0), (0, LANES - nh)))], axis=1).astype(BF16)
    tm = min(ROW_TILE, t)
    return pl.pallas_call(
        _inproj_kernel,
        grid=(t // tm,),
        in_specs=[
            pl.BlockSpec((tm, d), lambda i: (i, 0)),
            pl.BlockSpec((1, d), lambda i: (0, 0)),
            pl.BlockSpec((d, n + LANES), lambda i: (0, 0)),
        ],
        out_specs=[
            pl.BlockSpec((tm, n), lambda i: (i, 0)),
            pl.BlockSpec((tm, nh), lambda i: (i, 0)),
            pl.BlockSpec((nh, tm), lambda i: (0, i)),
        ],
        out_shape=[
            jax.ShapeDtypeStruct((t, n), BF16),
            jax.ShapeDtypeStruct((t, nh), F32),
            jax.ShapeDtypeStruct((nh, t), F32),
        ],
        compiler_params=_params(("parallel",)),
        name="inproj",
    )(x2, g.reshape(1, d), w_all)


def _attn_kernel(q_ref, k_ref, v_ref, o_ref, za_ref, zb_ref, wa_ref, wb_ref, bias_ref, *, tile, nq):
    i = pl.program_id(2)
    n_first = i + 1
    lane = lax.broadcasted_iota(jnp.int32, (tile, LANES), 1)
    row = lax.broadcasted_iota(jnp.int32, (tile, tile), 0)
    col = lax.broadcasted_iota(jnp.int32, (tile, tile), 1)
    m_incl = jnp.where(row >= col, 1.0, 0.0).astype(BF16)
    bias_ref[0] = jnp.zeros((tile, tile), F32)
    bias_ref[1] = jnp.where(col < row, 0.0, NEG_BIG)
    head_lanes = [(lane >= hh * HEAD_DIM) & (lane < (hh + 1) * HEAD_DIM) for hh in range(HEADS_PER_VREG)]

    def q_heads(t):
        q = q_ref[pl.ds(pl.multiple_of(t * tile, tile), tile), :].astype(F32) * (HEAD_DIM ** -0.5 * LOG2_E)
        return [jnp.where(m, q, 0.0).astype(BF16) for m in head_lanes]

    q_first, q_second = q_heads(i), q_heads(nq - 1 - i)

    def where_is(n):
        in_first = n < n_first
        t = jnp.where(in_first, i, nq - 1 - i)
        m = jnp.where(in_first, n, n - n_first)
        return in_first, t, t - m, m == 0

    def store_logits(n, z_ref):
        in_first, _, j, diag = where_is(n)
        ks = k_ref[pl.ds(pl.multiple_of(j * tile, tile), tile), :]
        bias = bias_ref[diag.astype(jnp.int32)]
        for hh in range(HEADS_PER_VREG):
            qh = jnp.where(in_first, q_first[hh], q_second[hh])
            z_ref[hh] = _dot_nt(qh, ks) + bias

    def store_weights(n, z_ref, w_ref, carries):
        diag = where_is(n)[3]
        new_carries = []
        for hh, carry in enumerate(carries):
            z2 = z_ref[hh]
            sp2 = jnp.maximum(jnp.log2(1.0 + jnp.exp2(jnp.minimum(z2, EXP2_CLAMP))), z2)
            cs2 = _dot(sp2.astype(BF16), m_incl) + jnp.where(diag, 0.0, carry)
            w = jnp.exp2(z_ref[hh] - cs2)
            w_ref[:, hh * tile:(hh + 1) * tile] = w.astype(BF16)
            new_carries.append(cs2[:, 0:1])
        return tuple(new_carries)

    def add_values(n, w_ref, acc):
        _, t, j, diag = where_is(n)
        vs = v_ref[pl.ds(pl.multiple_of(j * tile, tile), tile), :]
        v_heads = [jnp.where(m, vs, jnp.zeros_like(vs)) for m in head_lanes]
        acc = jnp.where(diag, 0.0, acc) + _dot(w_ref[...], jnp.concatenate(v_heads, axis=0))
        o_ref[pl.ds(pl.multiple_of(t * tile, tile), tile), :] = acc.astype(o_ref.dtype)
        return acc

    wb_ref[...] = jnp.zeros(wb_ref.shape, BF16)
    store_logits(0, za_ref)

    def pair(pp, st):
        carries, acc = st
        n = 2 * pp
        store_logits(n + 1, zb_ref)
        acc = add_values(jnp.maximum(n - 1, 0), wb_ref, acc)
        carries = store_weights(n, za_ref, wa_ref, carries)
        store_logits(n + 2, za_ref)
        acc = add_values(n, wa_ref, acc)
        carries = store_weights(n + 1, zb_ref, wb_ref, carries)
        return carries, acc

    zero = ((jnp.zeros((tile, 1), F32),) * HEADS_PER_VREG, jnp.zeros((tile, LANES), F32))
    carries, acc = lax.fori_loop(0, nq // 2, pair, zero, unroll=True)
    acc = add_values(nq - 1, wb_ref, acc)
    store_weights(nq, za_ref, wa_ref, carries)
    add_values(nq, wa_ref, acc)


def _attention(proj3, attn_width):
    b, s, _ = proj3.shape
    tile = min(ATTN_TILE, s // 2)
    nq = s // tile
    assert s % tile == 0 and nq % 2 == 0, "q tiles are processed in pairs (i, nq-1-i)"
    n_pairs = attn_width // LANES

    def whole_sequence(col):
        return pl.BlockSpec((None, s, LANES), lambda bi, p, i: (bi, 0, col * n_pairs + p))

    return pl.pallas_call(
        functools.partial(_attn_kernel, tile=tile, nq=nq),
        grid=(b, n_pairs, nq // 2),
        in_specs=[whole_sequence(0), whole_sequence(1), whole_sequence(2)],
        out_specs=whole_sequence(0),
        out_shape=jax.ShapeDtypeStruct((b, s, attn_width), BF16),
        scratch_shapes=[
            pltpu.VMEM((HEADS_PER_VREG, tile, tile), F32), pltpu.VMEM((HEADS_PER_VREG, tile, tile), F32),
            pltpu.VMEM((tile, HEADS_PER_VREG * tile), BF16), pltpu.VMEM((tile, HEADS_PER_VREG * tile), BF16),
            pltpu.VMEM((2, tile, tile), F32),
        ],
        compiler_params=_params(("parallel", "parallel", "arbitrary")),
        name="sb_attention",
    )(proj3, proj3, proj3)


def _ssd_kernel(xbc_ref, z_ref, dtc_ref, dtr_ref, cw_ref, cb_ref, dtb_c_ref, dtb_r_ref, alog_c_ref,
                alog_r_ref, dskip_ref, gn_ref, y_ref, ext_ref, state_ref, shift_ref, *, chunk, width, heads):
    c = pl.program_id(1)
    heads_per_group = heads // SSD_GROUPS
    group_width = heads_per_group * HEAD_DIM
    gs = SSD_GROUPS * SSD_STATE

    @pl.when(c == 0)
    def _():
        ext_ref[0:CONV_HALO, :] = jnp.zeros((CONV_HALO, ext_ref.shape[1]), F32)
        state_ref[...] = jnp.zeros(state_ref.shape, F32)
        row = lax.broadcasted_iota(jnp.int32, (chunk, chunk), 0)
        col = lax.broadcasted_iota(jnp.int32, (chunk, chunk), 1)
        for d in range(1, CONV_K):
            shift_ref[d - 1] = jnp.where(row - col == d, 1.0, 0.0).astype(BF16)

    u = xbc_ref[...]
    u32 = u.astype(F32)
    conv = cb_ref[...] + cw_ref[CONV_K - 1:CONV_K, :] * u32
    for d in range(1, CONV_K):
        conv = conv + cw_ref[CONV_K - 1 - d:CONV_K - d, :] * _dot(shift_ref[d - 1], u)
    ext_ref[CONV_HALO:2 * CONV_HALO, :] = u32[0:CONV_HALO]
    head = cb_ref[...]
    for k in range(CONV_K):
        head = head + cw_ref[k:k + 1, :] * ext_ref[pl.ds(CONV_HALO - (CONV_K - 1) + k, CONV_HALO), :]
    ext_ref[0:CONV_HALO, :] = u32[chunk - CONV_HALO:chunk]
    xbc = _silu(jnp.concatenate([head, conv[CONV_HALO:]], axis=0))
    xs = xbc[:, :width]
    bm = xbc[:, width:width + gs]
    cm = xbc[:, width + gs:]

    dt_c = _softplus(dtc_ref[...] + dtb_c_ref[...])
    dt_r = _softplus(dtr_ref[...] + dtb_r_ref[...])
    da_c = dt_c * (-jnp.exp(alog_c_ref[...]))
    da_r = dt_r * (-jnp.exp(alog_r_ref[...]))
    row = lax.broadcasted_iota(jnp.int32, (chunk, chunk), 0)
    col = lax.broadcasted_iota(jnp.int32, (chunk, chunk), 1)
    lower = row >= col
    tri_l = jnp.where(lower, 1.0, 0.0).astype(BF16)
    tri_u = jnp.where(row <= col, 1.0, 0.0).astype(BF16)
    cum_c = sum(_dot(tri_l, piece) for piece in _bf16_pieces(da_c, 3))
    cum_r = sum(_dot(piece, tri_u) for piece in _bf16_pieces(da_r, 3))
    last_c = cum_c[chunk - 1:chunk, :]
    last_r = cum_r[:, chunk - 1:chunk]
    tail_r = jnp.exp(last_r - cum_r)

    lane = lax.broadcasted_iota(jnp.int32, (chunk, LANES), 1)

    def per_head_lanes(a):
        first = lane[:a.shape[0]] < HEAD_DIM
        return jnp.concatenate([jnp.where(first, a[:, p:p + 1], a[:, p + 1:p + 2])
                                for p in range(0, heads, HEADS_PER_VREG)], axis=1)

    dt_x = per_head_lanes(dt_c)
    head_x = per_head_lanes(jnp.exp(cum_c))
    last_x = per_head_lanes(jnp.exp(last_c))
    xdt = xs * dt_x

    y_parts = []
    new_state = []
    for g in range(SSD_GROUPS):
        b_g = bm[:, g * SSD_STATE:(g + 1) * SSD_STATE]
        c_g = cm[:, g * SSD_STATE:(g + 1) * SSD_STATE]
        cb = _dot_nt(c_g.astype(BF16), b_g.astype(BF16))
        bt_g = b_g.T
        st_g = state_ref[:, g * group_width:(g + 1) * group_width]
        y_off = _dot(c_g.astype(BF16), st_g.astype(BF16))
        for p in range(group_width // LANES):
            lo = g * group_width + p * LANES
            xdt_p = xdt[:, lo:lo + LANES]
            y_p = jnp.zeros((chunk, LANES), F32)
            s_p = jnp.zeros((SSD_STATE, LANES), F32)
            for hh in range(HEADS_PER_VREG):
                h = lo // HEAD_DIM + hh
                head_lanes = (lane >= hh * HEAD_DIM) & (lane < (hh + 1) * HEAD_DIM)
                x_h = jnp.where(head_lanes, xdt_p, 0.0).astype(BF16)
                seg = cum_c[:, h:h + 1] - cum_r[h:h + 1, :]
                decay = jnp.exp(jnp.where(lower, seg, NEG_BIG))
                y_p = y_p + _dot((cb * decay).astype(BF16), x_h)
                s_p = s_p + _dot((bt_g * tail_r[h:h + 1, :]).astype(BF16), x_h)
            y_parts.append(y_p + y_off[:, p * LANES:(p + 1) * LANES] * head_x[:, lo:lo + LANES])
            new_state.append(s_p)
    y = jnp.concatenate(y_parts, axis=1) + dskip_ref[...] * xs
    state_ref[...] = state_ref[...] * last_x + jnp.concatenate(new_state, axis=1)

    y = y * _silu(z_ref[...].astype(F32))
    normed = []
    for g in range(SSD_GROUPS):
        yg = y[:, g * group_width:(g + 1) * group_width]
        normed.append(yg * lax.rsqrt(jnp.mean(yg * yg, axis=-1, keepdims=True) + EPS))
    y_ref[...] = (jnp.concatenate(normed, axis=1) * gn_ref[...]).astype(y_ref.dtype)


def _ssd(proj3, dtc3, dtr, conv_w, conv_b, dt_bias, a_log, d_skip, ssd_norm_g, attn_width):
    b, s, _ = proj3.shape
    heads = dt_bias.shape[0]
    width = heads * HEAD_DIM
    conv_dim = conv_w.shape[1]
    chunk = min(SSD_CHUNK, s)
    nc = s // chunk
    z_blk = 3 * attn_width // width
    xbc_blk = (3 * attn_width + width) // conv_dim
    small = lambda shape: pl.BlockSpec(shape, lambda bi, ci: (0, 0))
    return pl.pallas_call(
        functools.partial(_ssd_kernel, chunk=chunk, width=width, heads=heads),
        grid=(b, nc),
        in_specs=[
            pl.BlockSpec((None, chunk, conv_dim), lambda bi, ci: (bi, ci, xbc_blk)),
            pl.BlockSpec((None, chunk, width), lambda bi, ci: (bi, ci, z_blk)),
            pl.BlockSpec((None, chunk, heads), lambda bi, ci: (bi, ci, 0)),
            pl.BlockSpec((heads, chunk), lambda bi, ci: (0, bi * nc + ci)),
            small((CONV_K, conv_dim)), small((1, conv_dim)),
            small((1, heads)), small((heads, 1)), small((1, heads)), small((heads, 1)),
            small((1, width)), small((1, width)),
        ],
        out_specs=pl.BlockSpec((None, chunk, width), lambda bi, ci: (bi, ci, 0)),
        out_shape=jax.ShapeDtypeStruct((b, s, width), BF16),
        scratch_shapes=[
            pltpu.VMEM((2 * CONV_HALO, conv_dim), F32),
            pltpu.VMEM((SSD_STATE, width), F32),
            pltpu.VMEM((CONV_K - 1, chunk, chunk), BF16),
        ],
        compiler_params=_params(("parallel", "arbitrary")),
        name="conv_ssd",
    )(proj3, proj3, dtc3, dtr, conv_w, conv_b.reshape(1, conv_dim),
      dt_bias.reshape(1, heads), dt_bias.reshape(heads, 1), a_log.reshape(1, heads), a_log.reshape(heads, 1),
      jnp.repeat(d_skip, HEAD_DIM).reshape(1, width), ssd_norm_g.reshape(1, width))


def _outproj_kernel(attn_ref, y_ref, x_ref, ga_ref, wa_ref, wy_ref, gf_ref, x1_ref, h_ref):
    a = _rms(attn_ref[...].astype(F32), ga_ref[...]).astype(BF16)
    x1 = x_ref[...] + _dot(a, wa_ref[...]) + _dot(y_ref[...], wy_ref[...])
    x1_ref[...] = x1
    h_ref[...] = _rms(x1, gf_ref[...]).astype(h_ref.dtype)


def _outproj(attn2, y2, x2, attn_norm_g, w_out, ffn_norm_g):
    t, d = x2.shape
    wa = attn2.shape[1]
    wy = y2.shape[1]
    tm = min(ROW_TILE, t)
    w_bf = w_out.astype(BF16)
    return pl.pallas_call(
        _outproj_kernel,
        grid=(t // tm,),
        in_specs=[
            pl.BlockSpec((tm, wa), lambda i: (i, 0)),
            pl.BlockSpec((tm, wy), lambda i: (i, 0)),
            pl.BlockSpec((tm, d), lambda i: (i, 0)),
            pl.BlockSpec((1, wa), lambda i: (0, 0)),
            pl.BlockSpec((wa, d), lambda i: (0, 0)),
            pl.BlockSpec((wy, d), lambda i: (0, 0)),
            pl.BlockSpec((1, d), lambda i: (0, 0)),
        ],
        out_specs=[pl.BlockSpec((tm, d), lambda i: (i, 0)), pl.BlockSpec((tm, d), lambda i: (i, 0))],
        out_shape=[jax.ShapeDtypeStruct((t, d), F32), jax.ShapeDtypeStruct((t, d), BF16)],
        compiler_params=_params(("parallel",)),
        name="outproj",
    )(attn2, y2, x2, attn_norm_g.reshape(1, wa), w_bf[:wa], w_bf[wa:], ffn_norm_g.reshape(1, d))


def _swiglu(x, wg_ref, wu_ref, wd_ref):
    fc = wg_ref.shape[-1]
    sub = _ff_chunk(fc, FF_SUB)
    part = None
    for c in range(0, fc, sub):
        act = (_silu(_dot(x, wg_ref[:, c:c + sub])) * _dot(x, wu_ref[:, c:c + sub])).astype(BF16)
        piece = _dot(act, wd_ref[c:c + sub, :])
        part = piece if part is None else part + piece
    return part


def _outproj_ffn_kernel(attn_ref, y_ref, x_ref, ga_ref, wa_ref, wy_ref, gf_ref, wg_ref, wu_ref, wd_ref, gfin_ref,
                        o_ref, *, final_norm):
    a = _rms(attn_ref[...].astype(F32), ga_ref[...]).astype(BF16)
    x1 = x_ref[...] + _dot(a, wa_ref[...]) + _dot(y_ref[...], wy_ref[...])
    h = _rms(x1, gf_ref[...]).astype(BF16)
    out = x1 + _swiglu(h, wg_ref, wu_ref, wd_ref)
    o_ref[...] = _rms(out, gfin_ref[...]) if final_norm else out


def _ff_chunk(d_ff, target):
    best = LANES
    for c in range(LANES, min(d_ff, target) + 1, LANES):
        if d_ff % c == 0:
            best = c
    return best


def _outproj_dense_ffn(attn2, y2, x2, attn_norm_g, w_out, ffn_norm_g, wg, wu, wd, final_g, final_norm):
    t, d = x2.shape
    wa = attn2.shape[1]
    wy = y2.shape[1]
    d_ff = wg.shape[1]
    tm = min(ROW_TILE, t)
    w_bf = w_out.astype(BF16)
    row = lambda width: pl.BlockSpec((tm, width), lambda i: (i, 0))
    const = lambda shape: pl.BlockSpec(shape, lambda i: (0, 0))
    once = lambda shape: pl.BlockSpec(shape, lambda i: (0, 0), pipeline_mode=pl.Buffered(1))
    return pl.pallas_call(
        functools.partial(_outproj_ffn_kernel, final_norm=final_norm),
        grid=(t // tm,),
        in_specs=[
            row(wa), row(wy), row(d), const((1, wa)), once((wa, d)), once((wy, d)), const((1, d)),
            once((d, d_ff)), once((d, d_ff)), once((d_ff, d)), const((1, d)),
        ],
        out_specs=row(d),
        out_shape=jax.ShapeDtypeStruct((t, d), F32),
        compiler_params=_params(("parallel",)),
        name="outproj_dense_ffn",
    )(attn2, y2, x2, attn_norm_g.reshape(1, wa), w_bf[:wa], w_bf[wa:], ffn_norm_g.reshape(1, d),
      wg.astype(BF16), wu.astype(BF16), wd.astype(BF16), final_g.reshape(1, d))


def _router_kernel(x_ref, g_ref, rwt_ref, ri_ref, rg_ref, cum_ref, carry_ref, *, n_exp):
    blk = x_ref.shape[0]

    @pl.when(pl.program_id(0) == 0)
    def _():
        carry_ref[...] = jnp.zeros(carry_ref.shape, F32)

    h = _rms(x_ref[...], g_ref[...])
    w_hi, w_lo = _bf16_pieces(rwt_ref[...], 2)
    h_hi, h_lo = _bf16_pieces(h, 2)
    logits = _dot_nt(w_hi, h_hi) + _dot_nt(w_hi, h_lo) + _dot_nt(w_lo, h_hi)
    eidx = lax.broadcasted_iota(jnp.int32, (n_exp, blk), 0)
    m1 = jnp.max(logits, axis=0, keepdims=True)
    i1 = jnp.min(jnp.where(logits == m1, eidx, n_exp), axis=0, keepdims=True)
    rest = jnp.where(eidx == i1, -jnp.inf, logits)
    m2 = jnp.max(rest, axis=0, keepdims=True)
    i2 = jnp.min(jnp.where(rest == m2, eidx, n_exp), axis=0, keepdims=True)
    e21 = jnp.exp(m2 - m1)
    g1 = 1.0 / (1.0 + e21)
    g2 = e21 * g1

    sel1 = eidx == i1
    sel2 = eidx == i2
    onehot = jnp.where(sel1 | sel2, 1.0, 0.0)
    row = lax.broadcasted_iota(jnp.int32, (blk, blk), 0)
    col = lax.broadcasted_iota(jnp.int32, (blk, blk), 1)
    tri_u = jnp.where(row <= col, 1.0, 0.0).astype(BF16)
    incl = _dot(onehot.astype(BF16), tri_u)
    carry = carry_ref[:, 0:1]
    rank = carry + incl - onehot
    r1 = jnp.sum(jnp.where(sel1, rank, 0.0), axis=0, keepdims=True)
    r2 = jnp.sum(jnp.where(sel2, rank, 0.0), axis=0, keepdims=True)
    total = carry + jnp.sum(onehot, axis=1, keepdims=True)
    carry_ref[...] = jnp.broadcast_to(total, carry_ref.shape)
    cum_ref[...] = jnp.broadcast_to(total, cum_ref.shape).astype(jnp.int32)

    ri_ref[0:1, :] = i1
    ri_ref[1:2, :] = i2
    ri_ref[2:3, :] = r1.astype(jnp.int32)
    ri_ref[3:4, :] = r2.astype(jnp.int32)
    rg_ref[0:1, :] = g1
    rg_ref[1:2, :] = g2


def _router(x1, ffn_g, router_w):
    t, d = x1.shape
    n_exp = router_w.shape[1]
    blk = min(TOKEN_BLOCK, t)
    nb = t // blk
    return pl.pallas_call(
        functools.partial(_router_kernel, n_exp=n_exp),
        grid=(nb,),
        in_specs=[
            pl.BlockSpec((blk, d), lambda i: (i, 0)),
            pl.BlockSpec((1, d), lambda i: (0, 0)),
            pl.BlockSpec((n_exp, d), lambda i: (0, 0)),
        ],
        out_specs=[
            pl.BlockSpec((4, blk), lambda i: (0, i)),
            pl.BlockSpec((2, blk), lambda i: (0, i)),
            pl.BlockSpec((None, n_exp, LANES), lambda i: (i, 0, 0)),
        ],
        out_shape=[
            jax.ShapeDtypeStruct((4, t), jnp.int32),
            jax.ShapeDtypeStruct((2, t), F32),
            jax.ShapeDtypeStruct((nb, n_exp, LANES), jnp.int32),
        ],
        scratch_shapes=[pltpu.VMEM((n_exp, LANES), F32)],
        compiler_params=_params(("arbitrary",)),
        name="router",
    )(x1, ffn_g.reshape(1, d), router_w.T)


def _gather_kernel(b0_ref, b1_ref, pos_ref, gate_ref, h_ref, xs_ref, gs_ref, acc_ref, gacc_ref, *, tile, blk):
    i = pl.program_id(0)
    slot = i * tile + lax.broadcasted_iota(jnp.int32, (tile, 1), 0)
    acc_ref[...] = jnp.zeros(acc_ref.shape, F32)
    gacc_ref[...] = jnp.zeros(gacc_ref.shape, F32)

    def body(b, carry):
        pos = pos_ref[b]
        gate = gate_ref[b]
        hit1 = pos[0:1, :] == slot
        hit2 = pos[1:2, :] == slot
        onehot = jnp.where(hit1 | hit2, 1.0, 0.0).astype(BF16)
        start = pl.multiple_of(b * blk, blk)
        acc_ref[...] += _dot(onehot, h_ref[pl.ds(start, blk), :])
        gsel = jnp.where(hit1, gate[0:1, :], 0.0) + jnp.where(hit2, gate[1:2, :], 0.0)
        gacc_ref[...] += jnp.sum(gsel, axis=1, keepdims=True)
        return carry

    lax.fori_loop(b0_ref[i], b1_ref[i], body, 0)
    xs_ref[...] = acc_ref[...].astype(xs_ref.dtype)
    gs_ref[...] = gacc_ref[...]


def _gather(b0, b1, pos3, gate3, h2, n_tiles, tile):
    t, d = h2.shape
    nb, _, blk = pos3.shape
    grid_spec = pltpu.PrefetchScalarGridSpec(
        num_scalar_prefetch=2,
        grid=(n_tiles,),
        in_specs=[
            pl.BlockSpec((nb, 2, blk), lambda i, *_: (0, 0, 0)),
            pl.BlockSpec((nb, 2, blk), lambda i, *_: (0, 0, 0)),
            pl.BlockSpec((t, d), lambda i, *_: (0, 0), pipeline_mode=pl.Buffered(1)),
        ],
        out_specs=[
            pl.BlockSpec((tile, d), lambda i, *_: (i, 0)),
            pl.BlockSpec((tile, 1), lambda i, *_: (i, 0)),
        ],
        scratch_shapes=[pltpu.VMEM((tile, d), F32), pltpu.VMEM((tile, 1), F32)],
    )
    return pl.pallas_call(
        functools.partial(_gather_kernel, tile=tile, blk=blk),
        grid_spec=grid_spec,
        out_shape=[
            jax.ShapeDtypeStruct((n_tiles * tile, d), BF16),
            jax.ShapeDtypeStruct((n_tiles * tile, 1), F32),
        ],
        compiler_params=_params(("parallel",)),
        name="moe_gather",
    )(b0, b1, pos3, gate3, h2)


def _expert_kernel(te_ref, nv_ref, xs_ref, gs_ref, wg_ref, wu_ref, wd_ref, y_ref, acc_ref):
    i = pl.program_id(0)
    j = pl.program_id(1)
    valid = i < nv_ref[0]

    @pl.when((i == 0) & (j == 0))
    def _():
        acc_ref[...] = jnp.zeros(acc_ref.shape, F32)

    @pl.when(valid)
    def _():
        total = jnp.where(j == 0, 0.0, acc_ref[...]) + _swiglu(xs_ref[...], wg_ref, wu_ref, wd_ref)
        acc_ref[...] = total
        y_ref[...] = (total * gs_ref[...]).astype(y_ref.dtype)

    @pl.when(jnp.logical_not(valid))
    def _():
        y_ref[...] = jnp.zeros(y_ref.shape, y_ref.dtype)


def _experts(tile_expert, n_valid, xs, gs, wg, wu, wd, tile):
    p, d = xs.shape
    n_tiles = p // tile
    d_ff = wg.shape[2]
    fc = _ff_chunk(d_ff, 1792)
    nj = d_ff // fc

    def chunk(i, j, nv):
        return jnp.where(i < nv[0], j, nj - 1)

    grid_spec = pltpu.PrefetchScalarGridSpec(
        num_scalar_prefetch=2,
        grid=(n_tiles, nj),
        in_specs=[
            pl.BlockSpec((tile, d), lambda i, j, te, nv: (i, 0)),
            pl.BlockSpec((tile, 1), lambda i, j, te, nv: (i, 0)),
            pl.BlockSpec((None, d, fc), lambda i, j, te, nv: (te[i], 0, chunk(i, j, nv))),
            pl.BlockSpec((None, d, fc), lambda i, j, te, nv: (te[i], 0, chunk(i, j, nv))),
            pl.BlockSpec((None, fc, d), lambda i, j, te, nv: (te[i], chunk(i, j, nv), 0)),
        ],
        out_specs=pl.BlockSpec((tile, d), lambda i, j, te, nv: (i, 0)),
        scratch_shapes=[pltpu.VMEM((tile, d), F32)],
    )
    return pl.pallas_call(
        _expert_kernel,
        grid_spec=grid_spec,
        out_shape=jax.ShapeDtypeStruct((p, d), BF16),
        compiler_params=_params(("parallel", "arbitrary")),
        name="moe_experts",
    )(tile_expert, n_valid, xs, gs, wg.astype(BF16), wu.astype(BF16), wd.astype(BF16))


def _combine_kernel(ib_ref, it_ref, fl_ref, ys_ref, pos_ref, x_ref, gfin_ref, o_ref, acc_ref, *, tile, final_norm):
    w = pl.program_id(0)
    flags = fl_ref[w]

    @pl.when((flags & 1) != 0)
    def _():
        acc_ref[...] = x_ref[...]

    @pl.when((flags & 4) != 0)
    def _():
        slot = it_ref[w] * tile + lax.broadcasted_iota(jnp.int32, (1, tile), 1)
        pos = pos_ref[...]
        hit = (pos[:, 0:1] == slot) | (pos[:, 1:2] == slot)
        acc_ref[...] += _dot(jnp.where(hit, 1.0, 0.0).astype(BF16), ys_ref[...])

    @pl.when((flags & 2) != 0)
    def _():
        out = acc_ref[...]
        o_ref[...] = _rms(out, gfin_ref[...]) if final_norm else out


def _combine(item_block, item_tile, item_flags, ys, pos_col, x1, final_g, tile, final_norm):
    t, d = x1.shape
    blk = min(TOKEN_BLOCK, t)
    n_items = item_block.shape[0]
    grid_spec = pltpu.PrefetchScalarGridSpec(
        num_scalar_prefetch=3,
        grid=(n_items,),
        in_specs=[
            pl.BlockSpec((tile, d), lambda w, ib, it, fl: (it[w], 0)),
            pl.BlockSpec((blk, 2), lambda w, ib, it, fl: (ib[w], 0)),
            pl.BlockSpec((blk, d), lambda w, ib, it, fl: (ib[w], 0)),
            pl.BlockSpec((1, d), lambda w, ib, it, fl: (0, 0)),
        ],
        out_specs=pl.BlockSpec((blk, d), lambda w, ib, it, fl: (ib[w], 0)),
        scratch_shapes=[pltpu.VMEM((blk, d), F32)],
    )
    return pl.pallas_call(
        functools.partial(_combine_kernel, tile=tile, final_norm=final_norm),
        grid_spec=grid_spec,
        out_shape=jax.ShapeDtypeStruct((t, d), F32),
        compiler_params=_params(("arbitrary",)),
        name="moe_combine",
    )(item_block, item_tile, item_flags, ys, pos_col, x1, final_g.reshape(1, d))


def _moe(x1, h2, ffn_g, router_w, wg, wu, wd, final_g, final_norm):
    t, d = x1.shape
    n_exp = router_w.shape[1]
    blk = min(TOKEN_BLOCK, t)
    nb = t // blk
    tile = min(EXPERT_TILE, t)
    n_tiles = TOP_K * t // tile + n_exp
    i32 = jnp.int32

    ri, rg, cum_blocks = _router(x1, ffn_g, router_w)

    cum = jnp.concatenate([jnp.zeros((1, n_exp), i32), cum_blocks[:, :, 0]], axis=0)
    counts = cum[nb]
    tiles_per = (counts + tile - 1) // tile
    tile_end = jnp.cumsum(tiles_per)
    tile_start = tile_end - tiles_per
    seg_start = tile_start * tile
    n_valid = tile_end[n_exp - 1]
    tile_ids = jnp.arange(n_tiles, dtype=i32)
    tile_expert = jnp.minimum(jnp.sum(tile_ids[:, None] >= tile_end[None, :], axis=1), n_exp - 1).astype(i32)
    tile_expert = jnp.where(tile_ids < n_valid, tile_expert, tile_expert[jnp.maximum(n_valid - 1, 0)])

    seg_of = jnp.sum(jnp.where(ri[0:2, :, None] == jnp.arange(n_exp, dtype=i32), seg_start, 0), axis=-1)
    pos = seg_of + ri[2:4]
    pos3 = pos.reshape(2, nb, blk).transpose(1, 0, 2)
    gate3 = rg.reshape(2, nb, blk).transpose(1, 0, 2)

    r0 = (tile_ids - tile_start[tile_expert]) * tile
    cum_e = cum[:, tile_expert]
    b0 = jnp.sum(cum_e[1:] <= r0[None, :], axis=0)
    b1 = jnp.sum(cum_e[:-1] < (r0 + tile)[None, :], axis=0)
    live = tile_ids < n_valid
    b0 = jnp.where(live, b0, 0).astype(i32)
    b1 = jnp.where(live, b1, 0).astype(i32)

    xs, gs = _gather(b0, b1, pos3, gate3, h2, n_tiles, tile)
    ys = _experts(tile_expert, n_valid.reshape(1).astype(i32), xs, gs, wg, wu, wd, tile)

    lo = seg_start[None, :] + cum[:-1]
    hi = seg_start[None, :] + cum[1:]
    ctile = min(COMBINE_TILE, tile)
    first_tile = lo // ctile
    n_pair = jnp.where(hi > lo, (hi - 1) // ctile - first_tile + 1, 0).reshape(-1)
    pair_end = jnp.cumsum(n_pair)
    pair_start = pair_end - n_pair
    n_items = nb * n_exp + n_tiles * (tile // ctile)
    total = pair_end[-1]
    w_ids = jnp.arange(n_items, dtype=i32)
    w_clamped = jnp.minimum(w_ids, jnp.maximum(total - 1, 0))
    pair = jnp.minimum(jnp.sum(w_clamped[:, None] >= pair_end[None, :], axis=1), nb * n_exp - 1)
    item_block = (pair // n_exp).astype(i32)
    item_tile = (first_tile.reshape(-1)[pair] + w_clamped - pair_start[pair]).astype(i32)
    valid = w_ids < total
    prev_block = jnp.concatenate([jnp.full((1,), -1, i32), item_block[:-1]])
    next_block = jnp.concatenate([item_block[1:], jnp.full((1,), -1, i32)])
    is_first = valid & (item_block != prev_block)
    is_last = valid & ((item_block != next_block) | (w_ids == total - 1))
    item_flags = (is_first * 1 + is_last * 2 + valid * 4).astype(i32)

    return _combine(item_block, item_tile, item_flags, ys, pos.T, x1, final_g, ctile, final_norm)


def kernel(x, mix_norm_g, w_in, conv_w, conv_b, dt_bias, a_log, d_skip, attn_norm_g, ssd_norm_g, w_out, ffn_norm_g, dense_w_gate, dense_w_up, dense_w_down, router_w, moe_w_gate, moe_w_up, moe_w_down, final_norm_g):
    b, s, d = x.shape
    depth = w_in.shape[0]
    attn_width = attn_norm_g.shape[1]
    heads = dt_bias.shape[1]
    n_main = w_in.shape[2] - heads
    x2 = x.reshape(b * s, d)
    for layer in range(depth):
        proj, dtc, dtr = _inproj(x2, mix_norm_g[layer], w_in[layer, :, :n_main], w_in[layer, :, n_main:])
        proj3 = proj.reshape(b, s, n_main)
        attn = _attention(proj3, attn_width)
        y = _ssd(proj3, dtc.reshape(b, s, heads), dtr, conv_w[layer], conv_b[layer], dt_bias[layer],
                 a_log[layer], d_skip[layer], ssd_norm_g[layer], attn_width)
        attn2 = attn.reshape(b * s, attn_width)
        y2 = y.reshape(b * s, -1)
        final = layer == depth - 1
        i = layer // 2
        if layer % 2 == 0:
            x2 = _outproj_dense_ffn(attn2, y2, x2, attn_norm_g[layer], w_out[layer], ffn_norm_g[layer],
                                    dense_w_gate[i], dense_w_up[i], dense_w_down[i], final_norm_g, final)
        else:
            x1, h2 = _outproj(attn2, y2, x2, attn_norm_g[layer], w_out[layer], ffn_norm_g[layer])
            x2 = _moe(x1, h2, ffn_norm_g[layer], router_w[i], moe_w_gate[i], moe_w_up[i], moe_w_down[i],
                      final_norm_g, final)
    return x2.reshape(b, s, d)
```

```python
import functools

import jax
import jax.numpy as jnp
from jax import lax
from jax.experimental import pallas as pl
from jax.experimental.pallas import tpu as pltpu

F32 = jnp.float32
BF16 = jnp.bfloat16

HEAD_DIM = 64
HEADS_PER_VREG = 2
LANES = 128
SSD_GROUPS = 2
SSD_STATE = 128
CONV_K = 4
CONV_HALO = 8
TOP_K = 2
EPS = 1e-5
NEG_BIG = -1e30
LOG2_E = 1.4426950408889634
EXP2_CLAMP = 126.0

ROW_TILE = 512
FF_SUB = 256
ATTN_TILE = 256
SSD_CHUNK = 256
TOKEN_BLOCK = 512
EXPERT_TILE = 512
COMBINE_TILE = 256
VMEM_LIMIT = 56 * 1024 * 1024


def _params(semantics):
    return pltpu.CompilerParams(dimension_semantics=semantics, vmem_limit_bytes=VMEM_LIMIT)


def _rms(x, g):
    var = jnp.mean(x * x, axis=-1, keepdims=True)
    return x * lax.rsqrt(var + EPS) * g


def _silu(x):
    half = 0.5 * x
    return half + half * jnp.tanh(half)


def _softplus(x):
    return jnp.maximum(x, 0.0) + jnp.log1p(jnp.exp(-jnp.abs(x)))


def _dot(a, b):
    return jnp.dot(a, b, preferred_element_type=F32)


def _dot_nt(a, b):
    return lax.dot_general(a, b, (((1,), (1,)), ((), ())), preferred_element_type=F32)


def _bf16_pieces(x, n):
    pieces = []
    for _ in range(n):
        piece = x.astype(BF16)
        pieces.append(piece)
        x = x - piece.astype(F32)
    return pieces


def _inproj_kernel(x_ref, g_ref, w_ref, wdt_ref, proj_ref, dtc_ref, dtr_ref):
    h = _rms(x_ref[...], g_ref[...]).astype(BF16)
    proj_ref[...] = _dot(h, w_ref[...]).astype(proj_ref.dtype)
    dt_raw = _dot(h, wdt_ref[...])
    n_dt = dtc_ref.shape[-1]
    dtc_ref[...] = dt_raw[:, :n_dt]
    dtr_ref[...] = dt_raw.T[:n_dt, :]


def _inproj(x2, g, w_main, w_dt):
    t, d = x2.shape
    n = w_main.shape[1]
    nh = w_dt.shape[1]
    w_dt_pad = jnp.pad(w_dt, ((0, 0), (0, LANES - nh))).astype(BF16)
    tm = min(ROW_TILE, t)
    return pl.pallas_call(
        _inproj_kernel,
        grid=(t // tm,),
        in_specs=[
            pl.BlockSpec((tm, d), lambda i: (i, 0)),
            pl.BlockSpec((1, d), lambda i: (0, 0)),
            pl.BlockSpec((d, n), lambda i: (0, 0)),
            pl.BlockSpec((d, LANES), lambda i: (0, 0)),
        ],
        out_specs=[
            pl.BlockSpec((tm, n), lambda i: (i, 0)),
            pl.BlockSpec((tm, nh), lambda i: (i, 0)),
            pl.BlockSpec((nh, tm), lambda i: (0, i)),
        ],
        out_shape=[
            jax.ShapeDtypeStruct((t, n), BF16),
            jax.ShapeDtypeStruct((t, nh), F32),
            jax.ShapeDtypeStruct((nh, t), F32),
        ],
        compiler_params=_params(("parallel",)),
        name="inproj",
    )(x2, g.reshape(1, d), w_main.astype(BF16), w_dt_pad)


def _attn_kernel(q_ref, k_ref, v_ref, o_ref, za_ref, zb_ref, wa_ref, wb_ref, bias_ref, *, tile, nq):
    i = pl.program_id(2)
    n_first = i + 1
    lane = lax.broadcasted_iota(jnp.int32, (tile, LANES), 1)
    row = lax.broadcasted_iota(jnp.int32, (tile, tile), 0)
    col = lax.broadcasted_iota(jnp.int32, (tile, tile), 1)
    m_incl = jnp.where(row >= col, 1.0, 0.0).astype(BF16)
    bias_ref[0] = jnp.zeros((tile, tile), F32)
    bias_ref[1] = jnp.where(col < row, 0.0, NEG_BIG)
    head_lanes = [(lane >= hh * HEAD_DIM) & (lane < (hh + 1) * HEAD_DIM) for hh in range(HEADS_PER_VREG)]

    def q_heads(t):
        q = q_ref[pl.ds(pl.multiple_of(t * tile, tile), tile), :].astype(F32) * (HEAD_DIM ** -0.5 * LOG2_E)
        return [jnp.where(m, q, 0.0).astype(BF16) for m in head_lanes]

    q_first, q_second = q_heads(i), q_heads(nq - 1 - i)

    def where_is(n):
        if n == 0:
            return True, i, i, True
        if n > nq // 2:
            t = nq - 1 - i
            return False, t, t - (n - n_first), False
        in_first = n < n_first
        t = jnp.where(in_first, i, nq - 1 - i)
        m = jnp.where(in_first, n, n - n_first)
        return in_first, t, t - m, m == 0

    def pick(cond, a, b):
        return (a if cond else b) if isinstance(cond, bool) else jnp.where(cond, a, b)

    def store_logits(n, z_ref):
        in_first, _, j, diag = where_is(n)
        ks = k_ref[pl.ds(pl.multiple_of(j * tile, tile), tile), :]
        for hh in range(HEADS_PER_VREG):
            z2 = _dot_nt(pick(in_first, q_first[hh], q_second[hh]), ks)
            if diag is not False:
                z2 = z2 + bias_ref[1 if diag is True else diag.astype(jnp.int32)]
            z_ref[hh] = z2

    def store_weights(n, z_ref, w_ref, carries):
        diag = where_is(n)[3]
        new_carries = []
        for hh, carry in enumerate(carries):
            z2 = z_ref[hh]
            sp2 = jnp.maximum(jnp.log2(1.0 + jnp.exp2(jnp.minimum(z2, EXP2_CLAMP))), z2)
            cs2 = _dot(sp2.astype(BF16), m_incl)
            if diag is not True:
                cs2 = cs2 + pick(diag, 0.0, carry)
            w = jnp.exp2(z_ref[hh] - cs2)
            w_ref[:, hh * tile:(hh + 1) * tile] = w.astype(BF16)
            new_carries.append(cs2[:, 0:1])
        return tuple(new_carries)

    def add_values(n, w_ref, acc):
        _, t, j, diag = where_is(n)
        vs = v_ref[pl.ds(pl.multiple_of(j * tile, tile), tile), :]
        v_heads = [jnp.where(m, vs, jnp.zeros_like(vs)) for m in head_lanes]
        pv = _dot(w_ref[...], jnp.concatenate(v_heads, axis=0))
        acc = pv if diag is True else pick(diag, 0.0, acc) + pv
        o_ref[pl.ds(pl.multiple_of(t * tile, tile), tile), :] = acc.astype(o_ref.dtype)
        return acc

    store_logits(0, za_ref)
    carries = (jnp.zeros((tile, 1), F32),) * HEADS_PER_VREG
    acc = jnp.zeros((tile, LANES), F32)
    for n in range(0, nq, 2):
        store_logits(n + 1, zb_ref)
        if n > 0:
            acc = add_values(n - 1, wb_ref, acc)
        carries = store_weights(n, za_ref, wa_ref, carries)
        store_logits(n + 2, za_ref)
        acc = add_values(n, wa_ref, acc)
        carries = store_weights(n + 1, zb_ref, wb_ref, carries)
    acc = add_values(nq - 1, wb_ref, acc)
    store_weights(nq, za_ref, wa_ref, carries)
    add_values(nq, wa_ref, acc)


def _attention(proj3, attn_width):
    b, s, _ = proj3.shape
    tile = min(ATTN_TILE, s // 2)
    nq = s // tile
    assert s % tile == 0 and nq % 2 == 0, "q tiles are processed in pairs (i, nq-1-i)"
    n_pairs = attn_width // LANES

    def whole_sequence(col):
        return pl.BlockSpec((None, s, LANES), lambda bi, p, i: (bi, 0, col * n_pairs + p))

    return pl.pallas_call(
        functools.partial(_attn_kernel, tile=tile, nq=nq),
        grid=(b, n_pairs, nq // 2),
        in_specs=[whole_sequence(0), whole_sequence(1), whole_sequence(2)],
        out_specs=whole_sequence(0),
        out_shape=jax.ShapeDtypeStruct((b, s, attn_width), BF16),
        scratch_shapes=[
            pltpu.VMEM((HEADS_PER_VREG, tile, tile), F32), pltpu.VMEM((HEADS_PER_VREG, tile, tile), F32),
            pltpu.VMEM((tile, HEADS_PER_VREG * tile), BF16), pltpu.VMEM((tile, HEADS_PER_VREG * tile), BF16),
            pltpu.VMEM((2, tile, tile), F32),
        ],
        compiler_params=_params(("parallel", "parallel", "arbitrary")),
        name="sb_attention",
    )(proj3, proj3, proj3)


def _ssd_kernel(xbc_ref, z_ref, dtc_ref, dtr_ref, cw_ref, cb_ref, dtb_c_ref, dtb_r_ref, alog_c_ref,
                alog_r_ref, dskip_ref, gn_ref, y_ref, ext_ref, state_ref, shift_ref, *, chunk, width, heads):
    c = pl.program_id(1)
    heads_per_group = heads // SSD_GROUPS
    group_width = heads_per_group * HEAD_DIM
    gs = SSD_GROUPS * SSD_STATE

    @pl.when(c == 0)
    def _():
        ext_ref[0:CONV_HALO, :] = jnp.zeros((CONV_HALO, ext_ref.shape[1]), F32)
        state_ref[...] = jnp.zeros(state_ref.shape, F32)
        row = lax.broadcasted_iota(jnp.int32, (chunk, chunk), 0)
        col = lax.broadcasted_iota(jnp.int32, (chunk, chunk), 1)
        for d in range(1, CONV_K):
            shift_ref[d - 1] = jnp.where(row - col == d, 1.0, 0.0).astype(BF16)

    u = xbc_ref[...]
    u32 = u.astype(F32)
    conv = cb_ref[...] + cw_ref[CONV_K - 1:CONV_K, :] * u32
    for d in range(1, CONV_K):
        conv = conv + cw_ref[CONV_K - 1 - d:CONV_K - d, :] * _dot(shift_ref[d - 1], u)
    ext_ref[CONV_HALO:2 * CONV_HALO, :] = u32[0:CONV_HALO]
    head = cb_ref[...]
    for k in range(CONV_K):
        head = head + cw_ref[k:k + 1, :] * ext_ref[pl.ds(CONV_HALO - (CONV_K - 1) + k, CONV_HALO), :]
    ext_ref[0:CONV_HALO, :] = u32[chunk - CONV_HALO:chunk]
    xbc = _silu(jnp.concatenate([head, conv[CONV_HALO:]], axis=0))
    xs = xbc[:, :width]
    bm = xbc[:, width:width + gs]
    cm = xbc[:, width + gs:]

    dt_c = _softplus(dtc_ref[...] + dtb_c_ref[...])
    dt_r = _softplus(dtr_ref[...] + dtb_r_ref[...])
    da_c = dt_c * (-jnp.exp(alog_c_ref[...]))
    da_r = dt_r * (-jnp.exp(alog_r_ref[...]))
    row = lax.broadcasted_iota(jnp.int32, (chunk, chunk), 0)
    col = lax.broadcasted_iota(jnp.int32, (chunk, chunk), 1)
    lower = row >= col
    tri_l = jnp.where(lower, 1.0, 0.0).astype(BF16)
    tri_u = jnp.where(row <= col, 1.0, 0.0).astype(BF16)
    cum_c = sum(_dot(tri_l, piece) for piece in _bf16_pieces(da_c, 3))
    cum_r = sum(_dot(piece, tri_u) for piece in _bf16_pieces(da_r, 3))
    last_c = cum_c[chunk - 1:chunk, :]
    last_r = cum_r[:, chunk - 1:chunk]
    tail_r = jnp.exp(last_r - cum_r)

    lane = lax.broadcasted_iota(jnp.int32, (chunk, LANES), 1)

    def per_head_lanes(a):
        first = lane[:a.shape[0]] < HEAD_DIM
        return jnp.concatenate([jnp.where(first, a[:, p:p + 1], a[:, p + 1:p + 2])
                                for p in range(0, heads, HEADS_PER_VREG)], axis=1)

    dt_x = per_head_lanes(dt_c)
    head_x = per_head_lanes(jnp.exp(cum_c))
    last_x = per_head_lanes(jnp.exp(last_c))
    xdt = xs * dt_x

    y_parts = []
    new_state = []
    for g in range(SSD_GROUPS):
        b_g = bm[:, g * SSD_STATE:(g + 1) * SSD_STATE]
        c_g = cm[:, g * SSD_STATE:(g + 1) * SSD_STATE]
        cb = _dot_nt(c_g.astype(BF16), b_g.astype(BF16))
        bt_g = b_g.T
        st_g = state_ref[:, g * group_width:(g + 1) * group_width]
        y_off = _dot(c_g.astype(BF16), st_g.astype(BF16))
        for p in range(group_width // LANES):
            lo = g * group_width + p * LANES
            xdt_p = xdt[:, lo:lo + LANES]
            y_p = jnp.zeros((chunk, LANES), F32)
            s_p = jnp.zeros((SSD_STATE, LANES), F32)
            for hh in range(HEADS_PER_VREG):
                h = lo // HEAD_DIM + hh
                head_lanes = (lane >= hh * HEAD_DIM) & (lane < (hh + 1) * HEAD_DIM)
                x_h = jnp.where(head_lanes, xdt_p, 0.0).astype(BF16)
                seg = cum_c[:, h:h + 1] - cum_r[h:h + 1, :]
                decay = jnp.exp(jnp.where(lower, seg, NEG_BIG))
                y_p = y_p + _dot((cb * decay).astype(BF16), x_h)
                s_p = s_p + _dot((bt_g * tail_r[h:h + 1, :]).astype(BF16), x_h)
            y_parts.append(y_p + y_off[:, p * LANES:(p + 1) * LANES] * head_x[:, lo:lo + LANES])
            new_state.append(s_p)
    y = jnp.concatenate(y_parts, axis=1) + dskip_ref[...] * xs
    state_ref[...] = state_ref[...] * last_x + jnp.concatenate(new_state, axis=1)

    y = y * _silu(z_ref[...].astype(F32))
    normed = []
    for g in range(SSD_GROUPS):
        yg = y[:, g * group_width:(g + 1) * group_width]
        normed.append(yg * lax.rsqrt(jnp.mean(yg * yg, axis=-1, keepdims=True) + EPS))
    y_ref[...] = (jnp.concatenate(normed, axis=1) * gn_ref[...]).astype(y_ref.dtype)


def _ssd(proj3, dtc3, dtr, conv_w, conv_b, dt_bias, a_log, d_skip, ssd_norm_g, attn_width):
    b, s, _ = proj3.shape
    heads = dt_bias.shape[0]
    width = heads * HEAD_DIM
    conv_dim = conv_w.shape[1]
    chunk = min(SSD_CHUNK, s)
    nc = s // chunk
    z_blk = 3 * attn_width // width
    xbc_blk = (3 * attn_width + width) // conv_dim
    small = lambda shape: pl.BlockSpec(shape, lambda bi, ci: (0, 0))
    return pl.pallas_call(
        functools.partial(_ssd_kernel, chunk=chunk, width=width, heads=heads),
        grid=(b, nc),
        in_specs=[
            pl.BlockSpec((None, chunk, conv_dim), lambda bi, ci: (bi, ci, xbc_blk)),
            pl.BlockSpec((None, chunk, width), lambda bi, ci: (bi, ci, z_blk)),
            pl.BlockSpec((None, chunk, heads), lambda bi, ci: (bi, ci, 0)),
            pl.BlockSpec((heads, chunk), lambda bi, ci: (0, bi * nc + ci)),
            small((CONV_K, conv_dim)), small((1, conv_dim)),
            small((1, heads)), small((heads, 1)), small((1, heads)), small((heads, 1)),
            small((1, width)), small((1, width)),
        ],
        out_specs=pl.BlockSpec((None, chunk, width), lambda bi, ci: (bi, ci, 0)),
        out_shape=jax.ShapeDtypeStruct((b, s, width), BF16),
        scratch_shapes=[
            pltpu.VMEM((2 * CONV_HALO, conv_dim), F32),
            pltpu.VMEM((SSD_STATE, width), F32),
            pltpu.VMEM((CONV_K - 1, chunk, chunk), BF16),
        ],
        compiler_params=_params(("parallel", "arbitrary")),
        name="conv_ssd",
    )(proj3, proj3, dtc3, dtr, conv_w, conv_b.reshape(1, conv_dim),
      dt_bias.reshape(1, heads), dt_bias.reshape(heads, 1), a_log.reshape(1, heads), a_log.reshape(heads, 1),
      jnp.repeat(d_skip, HEAD_DIM).reshape(1, width), ssd_norm_g.reshape(1, width))


def _outproj_router_kernel(attn_ref, y_ref, x_ref, ga_ref, wa_ref, wy_ref, gf_ref, rwt_ref,
                           x1_ref, h_ref, ri_ref, rg_ref, cum_ref, carry_ref, *, n_exp):
    a = _rms(attn_ref[...].astype(F32), ga_ref[...]).astype(BF16)
    x1 = x_ref[...] + _dot(a, wa_ref[...]) + _dot(y_ref[...], wy_ref[...])
    x1_ref[...] = x1
    h = _rms(x1, gf_ref[...])
    h_ref[...] = h.astype(h_ref.dtype)
    _route(h, rwt_ref[...], ri_ref, rg_ref, cum_ref, carry_ref, n_exp)


def _outproj_router(attn2, y2, x2, attn_norm_g, w_out, ffn_norm_g, router_w):
    t, d = x2.shape
    wa = attn2.shape[1]
    wy = y2.shape[1]
    n_exp = router_w.shape[1]
    blk = min(TOKEN_BLOCK, t)
    nb = t // blk
    w_bf = w_out.astype(BF16)
    row = lambda width: pl.BlockSpec((blk, width), lambda i: (i, 0))
    const = lambda shape: pl.BlockSpec(shape, lambda i: (0, 0))
    return pl.pallas_call(
        functools.partial(_outproj_router_kernel, n_exp=n_exp),
        grid=(nb,),
        in_specs=[row(wa), row(wy), row(d), const((1, wa)), const((wa, d)), const((wy, d)), const((1, d)),
                  const((n_exp, d))],
        out_specs=[
            row(d), row(d),
            pl.BlockSpec((4, blk), lambda i: (0, i)),
            pl.BlockSpec((2, blk), lambda i: (0, i)),
            pl.BlockSpec((None, n_exp, LANES), lambda i: (i, 0, 0)),
        ],
        out_shape=[
            jax.ShapeDtypeStruct((t, d), F32), jax.ShapeDtypeStruct((t, d), BF16),
            jax.ShapeDtypeStruct((4, t), jnp.int32),
            jax.ShapeDtypeStruct((2, t), F32),
            jax.ShapeDtypeStruct((nb, n_exp, LANES), jnp.int32),
        ],
        scratch_shapes=[pltpu.VMEM((n_exp, LANES), F32)],
        compiler_params=_params(("arbitrary",)),
        name="outproj_router",
    )(attn2, y2, x2, attn_norm_g.reshape(1, wa), w_bf[:wa], w_bf[wa:], ffn_norm_g.reshape(1, d), router_w.T)


def _swiglu(x, wg_ref, wu_ref, wd_ref):
    fc = wg_ref.shape[-1]
    sub = _ff_chunk(fc, FF_SUB)
    part = None
    for c in range(0, fc, sub):
        act = (_silu(_dot(x, wg_ref[:, c:c + sub])) * _dot(x, wu_ref[:, c:c + sub])).astype(BF16)
        piece = _dot(act, wd_ref[c:c + sub, :])
        part = piece if part is None else part + piece
    return part


def _outproj_ffn_kernel(attn_ref, y_ref, x_ref, ga_ref, wa_ref, wy_ref, gf_ref, wg_ref, wu_ref, wd_ref, gfin_ref,
                        o_ref, *, final_norm):
    a = _rms(attn_ref[...].astype(F32), ga_ref[...]).astype(BF16)
    x1 = x_ref[...] + _dot(a, wa_ref[...]) + _dot(y_ref[...], wy_ref[...])
    h = _rms(x1, gf_ref[...]).astype(BF16)
    out = x1 + _swiglu(h, wg_ref, wu_ref, wd_ref)
    o_ref[...] = _rms(out, gfin_ref[...]) if final_norm else out


def _ff_chunk(d_ff, target):
    best = LANES
    for c in range(LANES, min(d_ff, target) + 1, LANES):
        if d_ff % c == 0:
            best = c
    return best


def _outproj_dense_ffn(attn2, y2, x2, attn_norm_g, w_out, ffn_norm_g, wg, wu, wd, final_g, final_norm):
    t, d = x2.shape
    wa = attn2.shape[1]
    wy = y2.shape[1]
    d_ff = wg.shape[1]
    tm = min(ROW_TILE, t)
    w_bf = w_out.astype(BF16)
    row = lambda width: pl.BlockSpec((tm, width), lambda i: (i, 0))
    const = lambda shape: pl.BlockSpec(shape, lambda i: (0, 0))
    once = lambda shape: pl.BlockSpec(shape, lambda i: (0, 0), pipeline_mode=pl.Buffered(1))
    return pl.pallas_call(
        functools.partial(_outproj_ffn_kernel, final_norm=final_norm),
        grid=(t // tm,),
        in_specs=[
            row(wa), row(wy), row(d), const((1, wa)), once((wa, d)), once((wy, d)), const((1, d)),
            once((d, d_ff)), once((d, d_ff)), once((d_ff, d)), const((1, d)),
        ],
        out_specs=row(d),
        out_shape=jax.ShapeDtypeStruct((t, d), F32),
        compiler_params=_params(("parallel",)),
        name="outproj_dense_ffn",
    )(attn2, y2, x2, attn_norm_g.reshape(1, wa), w_bf[:wa], w_bf[wa:], ffn_norm_g.reshape(1, d),
      wg.astype(BF16), wu.astype(BF16), wd.astype(BF16), final_g.reshape(1, d))


def _route(h, router_wt, ri_ref, rg_ref, cum_ref, carry_ref, n_exp):
    blk = h.shape[0]

    @pl.when(pl.program_id(0) == 0)
    def _():
        carry_ref[...] = jnp.zeros(carry_ref.shape, F32)

    w_hi, w_lo = _bf16_pieces(router_wt, 2)
    h_hi, h_lo = _bf16_pieces(h, 2)
    logits = _dot_nt(w_hi, h_hi) + _dot_nt(w_hi, h_lo) + _dot_nt(w_lo, h_hi)
    eidx = lax.broadcasted_iota(jnp.int32, (n_exp, blk), 0)
    m1 = jnp.max(logits, axis=0, keepdims=True)
    i1 = jnp.min(jnp.where(logits == m1, eidx, n_exp), axis=0, keepdims=True)
    rest = jnp.where(eidx == i1, -jnp.inf, logits)
    m2 = jnp.max(rest, axis=0, keepdims=True)
    i2 = jnp.min(jnp.where(rest == m2, eidx, n_exp), axis=0, keepdims=True)
    e21 = jnp.exp(m2 - m1)
    g1 = 1.0 / (1.0 + e21)
    g2 = e21 * g1

    sel1 = eidx == i1
    sel2 = eidx == i2
    onehot = jnp.where(sel1 | sel2, 1.0, 0.0)
    row = lax.broadcasted_iota(jnp.int32, (blk, blk), 0)
    col = lax.broadcasted_iota(jnp.int32, (blk, blk), 1)
    tri_u = jnp.where(row <= col, 1.0, 0.0).astype(BF16)
    incl = _dot(onehot.astype(BF16), tri_u)
    carry = carry_ref[:, 0:1]
    rank = carry + incl - onehot
    r1 = jnp.sum(jnp.where(sel1, rank, 0.0), axis=0, keepdims=True)
    r2 = jnp.sum(jnp.where(sel2, rank, 0.0), axis=0, keepdims=True)
    total = carry + jnp.sum(onehot, axis=1, keepdims=True)
    carry_ref[...] = jnp.broadcast_to(total, carry_ref.shape)
    cum_ref[...] = jnp.broadcast_to(total, cum_ref.shape).astype(jnp.int32)

    ri_ref[0:1, :] = i1
    ri_ref[1:2, :] = i2
    ri_ref[2:3, :] = r1.astype(jnp.int32)
    ri_ref[3:4, :] = r2.astype(jnp.int32)
    rg_ref[0:1, :] = g1
    rg_ref[1:2, :] = g2


def _gather_kernel(b0_ref, b1_ref, pos_ref, gate_ref, h_ref, xs_ref, gs_ref, acc_ref, gacc_ref, *, tile, blk):
    i = pl.program_id(0)
    slot = i * tile + lax.broadcasted_iota(jnp.int32, (tile, 1), 0)
    acc_ref[...] = jnp.zeros(acc_ref.shape, F32)
    gacc_ref[...] = jnp.zeros(gacc_ref.shape, F32)

    def body(b, carry):
        pos = pos_ref[b]
        gate = gate_ref[b]
        hit1 = pos[0:1, :] == slot
        hit2 = pos[1:2, :] == slot
        onehot = jnp.where(hit1 | hit2, 1.0, 0.0).astype(BF16)
        start = pl.multiple_of(b * blk, blk)
        acc_ref[...] += _dot(onehot, h_ref[pl.ds(start, blk), :])
        gsel = jnp.where(hit1, gate[0:1, :], 0.0) + jnp.where(hit2, gate[1:2, :], 0.0)
        gacc_ref[...] += jnp.sum(gsel, axis=1, keepdims=True)
        return carry

    lax.fori_loop(b0_ref[i], b1_ref[i], body, 0)
    xs_ref[...] = acc_ref[...].astype(xs_ref.dtype)
    gs_ref[...] = gacc_ref[...]


def _gather(b0, b1, pos3, gate3, h2, n_tiles, tile):
    t, d = h2.shape
    nb, _, blk = pos3.shape
    grid_spec = pltpu.PrefetchScalarGridSpec(
        num_scalar_prefetch=2,
        grid=(n_tiles,),
        in_specs=[
            pl.BlockSpec((nb, 2, blk), lambda i, *_: (0, 0, 0)),
            pl.BlockSpec((nb, 2, blk), lambda i, *_: (0, 0, 0)),
            pl.BlockSpec((t, d), lambda i, *_: (0, 0), pipeline_mode=pl.Buffered(1)),
        ],
        out_specs=[
            pl.BlockSpec((tile, d), lambda i, *_: (i, 0)),
            pl.BlockSpec((tile, 1), lambda i, *_: (i, 0)),
        ],
        scratch_shapes=[pltpu.VMEM((tile, d), F32), pltpu.VMEM((tile, 1), F32)],
    )
    return pl.pallas_call(
        functools.partial(_gather_kernel, tile=tile, blk=blk),
        grid_spec=grid_spec,
        out_shape=[
            jax.ShapeDtypeStruct((n_tiles * tile, d), BF16),
            jax.ShapeDtypeStruct((n_tiles * tile, 1), F32),
        ],
        compiler_params=_params(("parallel",)),
        name="moe_gather",
    )(b0, b1, pos3, gate3, h2)


def _expert_kernel(te_ref, nv_ref, xs_ref, gs_ref, wg_ref, wu_ref, wd_ref, y_ref, acc_ref):
    i = pl.program_id(0)
    j = pl.program_id(1)
    valid = i < nv_ref[0]

    @pl.when((i == 0) & (j == 0))
    def _():
        acc_ref[...] = jnp.zeros(acc_ref.shape, F32)

    @pl.when(valid)
    def _():
        total = jnp.where(j == 0, 0.0, acc_ref[...]) + _swiglu(xs_ref[...], wg_ref, wu_ref, wd_ref)
        acc_ref[...] = total
        y_ref[...] = (total * gs_ref[...]).astype(y_ref.dtype)

    @pl.when(jnp.logical_not(valid))
    def _():
        y_ref[...] = jnp.zeros(y_ref.shape, y_ref.dtype)


def _experts(tile_expert, n_valid, xs, gs, wg, wu, wd, tile):
    p, d = xs.shape
    n_tiles = p // tile
    d_ff = wg.shape[2]
    fc = _ff_chunk(d_ff, 1792)
    nj = d_ff // fc

    def chunk(i, j, nv):
        return jnp.where(i < nv[0], j, nj - 1)

    grid_spec = pltpu.PrefetchScalarGridSpec(
        num_scalar_prefetch=2,
        grid=(n_tiles, nj),
        in_specs=[
            pl.BlockSpec((tile, d), lambda i, j, te, nv: (i, 0)),
            pl.BlockSpec((tile, 1), lambda i, j, te, nv: (i, 0)),
            pl.BlockSpec((None, d, fc), lambda i, j, te, nv: (te[i], 0, chunk(i, j, nv))),
            pl.BlockSpec((None, d, fc), lambda i, j, te, nv: (te[i], 0, chunk(i, j, nv))),
            pl.BlockSpec((None, fc, d), lambda i, j, te, nv: (te[i], chunk(i, j, nv), 0)),
        ],
        out_specs=pl.BlockSpec((tile, d), lambda i, j, te, nv: (i, 0)),
        scratch_shapes=[pltpu.VMEM((tile, d), F32)],
    )
    return pl.pallas_call(
        _expert_kernel,
        grid_spec=grid_spec,
        out_shape=jax.ShapeDtypeStruct((p, d), BF16),
        compiler_params=_params(("parallel", "arbitrary")),
        name="moe_experts",
    )(tile_expert, n_valid, xs, gs, wg.astype(BF16), wu.astype(BF16), wd.astype(BF16))


def _combine_kernel(ib_ref, it_ref, fl_ref, ys_ref, pos_ref, x_ref, gfin_ref, o_ref, acc_ref, *, tile, final_norm):
    w = pl.program_id(0)
    flags = fl_ref[w]

    @pl.when((flags & 1) != 0)
    def _():
        acc_ref[...] = x_ref[...]

    @pl.when((flags & 4) != 0)
    def _():
        slot = it_ref[w] * tile + lax.broadcasted_iota(jnp.int32, (1, tile), 1)
        pos = pos_ref[...]
        hit = (pos[:, 0:1] == slot) | (pos[:, 1:2] == slot)
        acc_ref[...] += _dot(jnp.where(hit, 1.0, 0.0).astype(BF16), ys_ref[...])

    @pl.when((flags & 2) != 0)
    def _():
        out = acc_ref[...]
        o_ref[...] = _rms(out, gfin_ref[...]) if final_norm else out


def _combine(item_block, item_tile, item_flags, ys, pos_col, x1, final_g, tile, final_norm):
    t, d = x1.shape
    blk = min(TOKEN_BLOCK, t)
    n_items = item_block.shape[0]
    grid_spec = pltpu.PrefetchScalarGridSpec(
        num_scalar_prefetch=3,
        grid=(n_items,),
        in_specs=[
            pl.BlockSpec((tile, d), lambda w, ib, it, fl: (it[w], 0)),
            pl.BlockSpec((blk, 2), lambda w, ib, it, fl: (ib[w], 0)),
            pl.BlockSpec((blk, d), lambda w, ib, it, fl: (ib[w], 0)),
            pl.BlockSpec((1, d), lambda w, ib, it, fl: (0, 0)),
        ],
        out_specs=pl.BlockSpec((blk, d), lambda w, ib, it, fl: (ib[w], 0)),
        scratch_shapes=[pltpu.VMEM((blk, d), F32)],
    )
    return pl.pallas_call(
        functools.partial(_combine_kernel, tile=tile, final_norm=final_norm),
        grid_spec=grid_spec,
        out_shape=jax.ShapeDtypeStruct((t, d), F32),
        compiler_params=_params(("arbitrary",)),
        name="moe_combine",
    )(item_block, item_tile, item_flags, ys, pos_col, x1, final_g.reshape(1, d))


def _moe(x1, h2, ri, rg, cum_blocks, wg, wu, wd, final_g, final_norm):
    t, d = x1.shape
    n_exp = wg.shape[0]
    blk = min(TOKEN_BLOCK, t)
    nb = t // blk
    tile = min(EXPERT_TILE, t)
    n_tiles = TOP_K * t // tile + n_exp
    i32 = jnp.int32

    cum = jnp.concatenate([jnp.zeros((1, n_exp), i32), cum_blocks[:, :, 0]], axis=0)
    counts = cum[nb]
    tiles_per = (counts + tile - 1) // tile
    tile_end = jnp.cumsum(tiles_per)
    tile_start = tile_end - tiles_per
    seg_start = tile_start * tile
    n_valid = tile_end[n_exp - 1]
    tile_ids = jnp.arange(n_tiles, dtype=i32)
    tile_expert = jnp.minimum(jnp.sum(tile_ids[:, None] >= tile_end[None, :], axis=1), n_exp - 1).astype(i32)
    tile_expert = jnp.where(tile_ids < n_valid, tile_expert, tile_expert[jnp.maximum(n_valid - 1, 0)])

    seg_of = jnp.sum(jnp.where(ri[0:2, :, None] == jnp.arange(n_exp, dtype=i32), seg_start, 0), axis=-1)
    pos = seg_of + ri[2:4]
    pos3 = pos.reshape(2, nb, blk).transpose(1, 0, 2)
    gate3 = rg.reshape(2, nb, blk).transpose(1, 0, 2)

    r0 = (tile_ids - tile_start[tile_expert]) * tile
    cum_e = cum[:, tile_expert]
    b0 = jnp.sum(cum_e[1:] <= r0[None, :], axis=0)
    b1 = jnp.sum(cum_e[:-1] < (r0 + tile)[None, :], axis=0)
    live = tile_ids < n_valid
    b0 = jnp.where(live, b0, 0).astype(i32)
    b1 = jnp.where(live, b1, 0).astype(i32)

    xs, gs = _gather(b0, b1, pos3, gate3, h2, n_tiles, tile)
    ys = _experts(tile_expert, n_valid.reshape(1).astype(i32), xs, gs, wg, wu, wd, tile)

    lo = seg_start[None, :] + cum[:-1]
    hi = seg_start[None, :] + cum[1:]
    ctile = min(COMBINE_TILE, tile)
    first_tile = lo // ctile
    n_pair = jnp.where(hi > lo, (hi - 1) // ctile - first_tile + 1, 0).reshape(-1)
    pair_end = jnp.cumsum(n_pair)
    pair_start = pair_end - n_pair
    n_items = nb * n_exp + n_tiles * (tile // ctile)
    total = pair_end[-1]
    w_ids = jnp.arange(n_items, dtype=i32)
    w_clamped = jnp.minimum(w_ids, jnp.maximum(total - 1, 0))
    pair = jnp.minimum(jnp.sum(w_clamped[:, None] >= pair_end[None, :], axis=1), nb * n_exp - 1)
    item_block = (pair // n_exp).astype(i32)
    item_tile = (first_tile.reshape(-1)[pair] + w_clamped - pair_start[pair]).astype(i32)
    valid = w_ids < total
    prev_block = jnp.concatenate([jnp.full((1,), -1, i32), item_block[:-1]])
    next_block = jnp.concatenate([item_block[1:], jnp.full((1,), -1, i32)])
    is_first = valid & (item_block != prev_block)
    is_last = valid & ((item_block != next_block) | (w_ids == total - 1))
    item_flags = (is_first * 1 + is_last * 2 + valid * 4).astype(i32)

    return _combine(item_block, item_tile, item_flags, ys, pos.T, x1, final_g, ctile, final_norm)


def kernel(x, mix_norm_g, w_in, conv_w, conv_b, dt_bias, a_log, d_skip, attn_norm_g, ssd_norm_g, w_out, ffn_norm_g, dense_w_gate, dense_w_up, dense_w_down, router_w, moe_w_gate, moe_w_up, moe_w_down, final_norm_g):
    b, s, d = x.shape
    depth = w_in.shape[0]
    attn_width = attn_norm_g.shape[1]
    heads = dt_bias.shape[1]
    n_main = w_in.shape[2] - heads
    x2 = x.reshape(b * s, d)
    for layer in range(depth):
        proj, dtc, dtr = _inproj(x2, mix_norm_g[layer], w_in[layer, :, :n_main], w_in[layer, :, n_main:])
        proj3 = proj.reshape(b, s, n_main)
        attn = _attention(proj3, attn_width)
        y = _ssd(proj3, dtc.reshape(b, s, heads), dtr, conv_w[layer], conv_b[layer], dt_bias[layer],
                 a_log[layer], d_skip[layer], ssd_norm_g[layer], attn_width)
        attn2 = attn.reshape(b * s, attn_width)
        y2 = y.reshape(b * s, -1)
        final = layer == depth - 1
        i = layer // 2
        if layer % 2 == 0:
            x2 = _outproj_dense_ffn(attn2, y2, x2, attn_norm_g[layer], w_out[layer], ffn_norm_g[layer],
                                    dense_w_gate[i], dense_w_up[i], dense_w_down[i], final_norm_g, final)
        else:
            x1, h2, ri, rg, cum_blocks = _outproj_router(attn2, y2, x2, attn_norm_g[layer], w_out[layer],
                                                         ffn_norm_g[layer], router_w[i])
            x2 = _moe(x1, h2, ri, rg, cum_blocks, moe_w_gate[i], moe_w_up[i], moe_w_down[i],
                      final_norm_g, final)
    return x2.reshape(b, s, d)
```

```python
import functools

import jax
import jax.numpy as jnp
from jax import lax
from jax.experimental import pallas as pl
from jax.experimental.pallas import tpu as pltpu

F32 = jnp.float32
BF16 = jnp.bfloat16

HEAD_DIM = 64
HEADS_PER_VREG = 2
LANES = 128
SSD_GROUPS = 2
SSD_STATE = 128
CONV_K = 4
CONV_HALO = 8
TOP_K = 2
EPS = 1e-5
NEG_BIG = -1e30
LOG2_E = 1.4426950408889634
EXP2_CLAMP = 126.0

ROW_TILE = 512
FF_SUB = 256
ATTN_TILE = 256
SSD_CHUNK = 256
TOKEN_BLOCK = 512
EXPERT_TILE = 512
COMBINE_TILE = 256
VMEM_LIMIT = 56 * 1024 * 1024


def _params(semantics):
    return pltpu.CompilerParams(dimension_semantics=semantics, vmem_limit_bytes=VMEM_LIMIT)


def _rms(x, g):
    var = jnp.mean(x * x, axis=-1, keepdims=True)
    return x * lax.rsqrt(var + EPS) * g


def _silu(x):
    half = 0.5 * x
    return half + half * jnp.tanh(half)


def _softplus(x):
    return jnp.maximum(x, 0.0) + jnp.log1p(jnp.exp(-jnp.abs(x)))


def _dot(a, b):
    return jnp.dot(a, b, preferred_element_type=F32)


def _dot_nt(a, b):
    return lax.dot_general(a, b, (((1,), (1,)), ((), ())), preferred_element_type=F32)


def _bf16_pieces(x, n):
    pieces = []
    for _ in range(n):
        piece = x.astype(BF16)
        pieces.append(piece)
        x = x - piece.astype(F32)
    return pieces


def _inproj_kernel(x_ref, g_ref, w_ref, wdt_ref, proj_ref, dtc_ref, dtr_ref):
    h = _rms(x_ref[...], g_ref[...]).astype(BF16)
    proj_ref[...] = _dot(h, w_ref[...]).astype(proj_ref.dtype)
    dt_raw = _dot(h, wdt_ref[...])
    n_dt = dtc_ref.shape[-1]
    dtc_ref[...] = dt_raw[:, :n_dt]
    dtr_ref[...] = dt_raw.T[:n_dt, :]


def _inproj(x2, g, w_main, w_dt):
    t, d = x2.shape
    n = w_main.shape[1]
    nh = w_dt.shape[1]
    w_dt_pad = jnp.pad(w_dt, ((0, 0), (0, LANES - nh))).astype(BF16)
    tm = min(ROW_TILE, t)
    return pl.pallas_call(
        _inproj_kernel,
        grid=(t // tm,),
        in_specs=[
            pl.BlockSpec((tm, d), lambda i: (i, 0)),
            pl.BlockSpec((1, d), lambda i: (0, 0)),
            pl.BlockSpec((d, n), lambda i: (0, 0)),
            pl.BlockSpec((d, LANES), lambda i: (0, 0)),
        ],
        out_specs=[
            pl.BlockSpec((tm, n), lambda i: (i, 0)),
            pl.BlockSpec((tm, nh), lambda i: (i, 0)),
            pl.BlockSpec((nh, tm), lambda i: (0, i)),
        ],
        out_shape=[
            jax.ShapeDtypeStruct((t, n), BF16),
            jax.ShapeDtypeStruct((t, nh), F32),
            jax.ShapeDtypeStruct((nh, t), F32),
        ],
        compiler_params=_params(("parallel",)),
        name="inproj",
    )(x2, g.reshape(1, d), w_main.astype(BF16), w_dt_pad)


def _attn_kernel(q_ref, k_ref, v_ref, o_ref, za_ref, zb_ref, wa_ref, wb_ref, bias_ref, *, tile, nq):
    i = pl.program_id(2)
    n_first = i + 1
    lane = lax.broadcasted_iota(jnp.int32, (tile, LANES), 1)
    row = lax.broadcasted_iota(jnp.int32, (tile, tile), 0)
    col = lax.broadcasted_iota(jnp.int32, (tile, tile), 1)
    m_incl = jnp.where(row >= col, 1.0, 0.0).astype(BF16)
    bias_ref[0] = jnp.zeros((tile, tile), F32)
    bias_ref[1] = jnp.where(col < row, 0.0, NEG_BIG)
    head_lanes = [(lane >= hh * HEAD_DIM) & (lane < (hh + 1) * HEAD_DIM) for hh in range(HEADS_PER_VREG)]

    def q_heads(t):
        q = q_ref[pl.ds(pl.multiple_of(t * tile, tile), tile), :].astype(F32) * (HEAD_DIM ** -0.5 * LOG2_E)
        return [jnp.where(m, q, 0.0).astype(BF16) for m in head_lanes]

    q_first, q_second = q_heads(i), q_heads(nq - 1 - i)

    def where_is(n):
        if n == 0:
            return True, i, i, True
        if n > nq // 2:
            t = nq - 1 - i
            return False, t, t - (n - n_first), False
        in_first = n < n_first
        t = jnp.where(in_first, i, nq - 1 - i)
        m = jnp.where(in_first, n, n - n_first)
        return in_first, t, t - m, m == 0

    def pick(cond, a, b):
        return (a if cond else b) if isinstance(cond, bool) else jnp.where(cond, a, b)

    def store_logits(n, z_ref):
        in_first, _, j, diag = where_is(n)
        ks = k_ref[pl.ds(pl.multiple_of(j * tile, tile), tile), :]
        for hh in range(HEADS_PER_VREG):
            z2 = _dot_nt(pick(in_first, q_first[hh], q_second[hh]), ks)
            if diag is not False:
                z2 = z2 + bias_ref[1 if diag is True else diag.astype(jnp.int32)]
            z_ref[hh] = z2

    def store_weights(n, z_ref, w_ref, carries):
        diag = where_is(n)[3]
        new_carries = []
        for hh, carry in enumerate(carries):
            z2 = z_ref[hh]
            sp2 = jnp.maximum(jnp.log2(1.0 + jnp.exp2(jnp.minimum(z2, EXP2_CLAMP))), z2)
            cs2 = _dot(sp2.astype(BF16), m_incl)
            if diag is not True:
                cs2 = cs2 + pick(diag, 0.0, carry)
            w = jnp.exp2(z_ref[hh] - cs2)
            w_ref[:, hh * tile:(hh + 1) * tile] = w.astype(BF16)
            new_carries.append(cs2[:, 0:1])
        return tuple(new_carries)

    def add_values(n, w_ref, acc):
        _, t, j, diag = where_is(n)
        vs = v_ref[pl.ds(pl.multiple_of(j * tile, tile), tile), :]
        v_heads = [jnp.where(m, vs, jnp.zeros_like(vs)) for m in head_lanes]
        pv = _dot(w_ref[...], jnp.concatenate(v_heads, axis=0))
        acc = pv if diag is True else pick(diag, 0.0, acc) + pv
        o_ref[pl.ds(pl.multiple_of(t * tile, tile), tile), :] = acc.astype(o_ref.dtype)
        return acc

    store_logits(0, za_ref)
    carries = (jnp.zeros((tile, 1), F32),) * HEADS_PER_VREG
    acc = jnp.zeros((tile, LANES), F32)
    for n in range(0, nq, 2):
        store_logits(n + 1, zb_ref)
        if n > 0:
            acc = add_values(n - 1, wb_ref, acc)
        carries = store_weights(n, za_ref, wa_ref, carries)
        store_logits(n + 2, za_ref)
        acc = add_values(n, wa_ref, acc)
        carries = store_weights(n + 1, zb_ref, wb_ref, carries)
    acc = add_values(nq - 1, wb_ref, acc)
    store_weights(nq, za_ref, wa_ref, carries)
    add_values(nq, wa_ref, acc)


def _attention(proj3, attn_width):
    b, s, _ = proj3.shape
    tile = min(ATTN_TILE, s // 2)
    nq = s // tile
    assert s % tile == 0 and nq % 2 == 0, "q tiles are processed in pairs (i, nq-1-i)"
    n_pairs = attn_width // LANES

    def whole_sequence(col):
        return pl.BlockSpec((None, s, LANES), lambda bi, p, i: (bi, 0, col * n_pairs + p))

    return pl.pallas_call(
        functools.partial(_attn_kernel, tile=tile, nq=nq),
        grid=(b, n_pairs, nq // 2),
        in_specs=[whole_sequence(0), whole_sequence(1), whole_sequence(2)],
        out_specs=whole_sequence(0),
        out_shape=jax.ShapeDtypeStruct((b, s, attn_width), BF16),
        scratch_shapes=[
            pltpu.VMEM((HEADS_PER_VREG, tile, tile), F32), pltpu.VMEM((HEADS_PER_VREG, tile, tile), F32),
            pltpu.VMEM((tile, HEADS_PER_VREG * tile), BF16), pltpu.VMEM((tile, HEADS_PER_VREG * tile), BF16),
            pltpu.VMEM((2, tile, tile), F32),
        ],
        compiler_params=_params(("parallel", "parallel", "arbitrary")),
        name="sb_attention",
    )(proj3, proj3, proj3)


def _ssd_kernel(xbc_ref, z_ref, dtc_ref, dtr_ref, cw_ref, cb_ref, dtb_c_ref, dtb_r_ref, alog_c_ref,
                alog_r_ref, dskip_ref, gn_ref, y_ref, ext_ref, state_ref, shift_ref, *, chunk, width, heads):
    c = pl.program_id(1)
    heads_per_group = heads // SSD_GROUPS
    group_width = heads_per_group * HEAD_DIM
    gs = SSD_GROUPS * SSD_STATE

    @pl.when(c == 0)
    def _():
        ext_ref[0:CONV_HALO, :] = jnp.zeros((CONV_HALO, ext_ref.shape[1]), F32)
        state_ref[...] = jnp.zeros(state_ref.shape, F32)
        row = lax.broadcasted_iota(jnp.int32, (chunk, chunk), 0)
        col = lax.broadcasted_iota(jnp.int32, (chunk, chunk), 1)
        for d in range(1, CONV_K):
            shift_ref[d - 1] = jnp.where(row - col == d, 1.0, 0.0).astype(BF16)

    u = xbc_ref[...]
    u32 = u.astype(F32)
    conv = cb_ref[...] + cw_ref[CONV_K - 1:CONV_K, :] * u32
    for d in range(1, CONV_K):
        conv = conv + cw_ref[CONV_K - 1 - d:CONV_K - d, :] * _dot(shift_ref[d - 1], u)
    ext_ref[CONV_HALO:2 * CONV_HALO, :] = u32[0:CONV_HALO]
    head = cb_ref[...]
    for k in range(CONV_K):
        head = head + cw_ref[k:k + 1, :] * ext_ref[pl.ds(CONV_HALO - (CONV_K - 1) + k, CONV_HALO), :]
    ext_ref[0:CONV_HALO, :] = u32[chunk - CONV_HALO:chunk]
    xbc = _silu(jnp.concatenate([head, conv[CONV_HALO:]], axis=0))
    xs = xbc[:, :width]
    bm = xbc[:, width:width + gs]
    cm = xbc[:, width + gs:]

    dt_c = _softplus(dtc_ref[...] + dtb_c_ref[...])
    dt_r = _softplus(dtr_ref[...] + dtb_r_ref[...])
    da_c = dt_c * (-jnp.exp(alog_c_ref[...]))
    da_r = dt_r * (-jnp.exp(alog_r_ref[...]))
    row = lax.broadcasted_iota(jnp.int32, (chunk, chunk), 0)
    col = lax.broadcasted_iota(jnp.int32, (chunk, chunk), 1)
    lower = row >= col
    tri_l = jnp.where(lower, 1.0, 0.0).astype(BF16)
    tri_u = jnp.where(row <= col, 1.0, 0.0).astype(BF16)
    cum_c = sum(_dot(tri_l, piece) for piece in _bf16_pieces(da_c, 3))
    cum_r = sum(_dot(piece, tri_u) for piece in _bf16_pieces(da_r, 3))
    last_c = cum_c[chunk - 1:chunk, :]
    last_r = cum_r[:, chunk - 1:chunk]
    tail_r = jnp.exp(last_r - cum_r)

    lane = lax.broadcasted_iota(jnp.int32, (chunk, LANES), 1)

    def per_head_lanes(a):
        first = lane[:a.shape[0]] < HEAD_DIM
        return jnp.concatenate([jnp.where(first, a[:, p:p + 1], a[:, p + 1:p + 2])
                                for p in range(0, heads, HEADS_PER_VREG)], axis=1)

    dt_x = per_head_lanes(dt_c)
    head_x = per_head_lanes(jnp.exp(cum_c))
    last_x = per_head_lanes(jnp.exp(last_c))
    xdt = xs * dt_x

    y_parts = []
    new_state = []
    for g in range(SSD_GROUPS):
        b_g = bm[:, g * SSD_STATE:(g + 1) * SSD_STATE]
        c_g = cm[:, g * SSD_STATE:(g + 1) * SSD_STATE]
        cb = _dot_nt(c_g.astype(BF16), b_g.astype(BF16))
        bt_g = b_g.T
        st_g = state_ref[:, g * group_width:(g + 1) * group_width]
        y_off = _dot(c_g.astype(BF16), st_g.astype(BF16))
        for p in range(group_width // LANES):
            lo = g * group_width + p * LANES
            xdt_p = xdt[:, lo:lo + LANES]
            y_p = jnp.zeros((chunk, LANES), F32)
            s_p = jnp.zeros((SSD_STATE, LANES), F32)
            for hh in range(HEADS_PER_VREG):
                h = lo // HEAD_DIM + hh
                head_lanes = (lane >= hh * HEAD_DIM) & (lane < (hh + 1) * HEAD_DIM)
                x_h = jnp.where(head_lanes, xdt_p, 0.0).astype(BF16)
                seg = cum_c[:, h:h + 1] - cum_r[h:h + 1, :]
                decay = jnp.exp(jnp.where(lower, seg, NEG_BIG))
                y_p = y_p + _dot((cb * decay).astype(BF16), x_h)
                s_p = s_p + _dot((bt_g * tail_r[h:h + 1, :]).astype(BF16), x_h)
            y_parts.append(y_p + y_off[:, p * LANES:(p + 1) * LANES] * head_x[:, lo:lo + LANES])
            new_state.append(s_p)
    y = jnp.concatenate(y_parts, axis=1) + dskip_ref[...] * xs
    state_ref[...] = state_ref[...] * last_x + jnp.concatenate(new_state, axis=1)

    y = y * _silu(z_ref[...].astype(F32))
    normed = []
    for g in range(SSD_GROUPS):
        yg = y[:, g * group_width:(g + 1) * group_width]
        normed.append(yg * lax.rsqrt(jnp.mean(yg * yg, axis=-1, keepdims=True) + EPS))
    y_ref[...] = (jnp.concatenate(normed, axis=1) * gn_ref[...]).astype(y_ref.dtype)


def _ssd(proj3, dtc3, dtr, conv_w, conv_b, dt_bias, a_log, d_skip, ssd_norm_g, attn_width):
    b, s, _ = proj3.shape
    heads = dt_bias.shape[0]
    width = heads * HEAD_DIM
    conv_dim = conv_w.shape[1]
    chunk = min(SSD_CHUNK, s)
    nc = s // chunk
    z_blk = 3 * attn_width // width
    xbc_blk = (3 * attn_width + width) // conv_dim
    small = lambda shape: pl.BlockSpec(shape, lambda bi, ci: (0, 0))
    return pl.pallas_call(
        functools.partial(_ssd_kernel, chunk=chunk, width=width, heads=heads),
        grid=(b, nc),
        in_specs=[
            pl.BlockSpec((None, chunk, conv_dim), lambda bi, ci: (bi, ci, xbc_blk)),
            pl.BlockSpec((None, chunk, width), lambda bi, ci: (bi, ci, z_blk)),
            pl.BlockSpec((None, chunk, heads), lambda bi, ci: (bi, ci, 0)),
            pl.BlockSpec((heads, chunk), lambda bi, ci: (0, bi * nc + ci)),
            small((CONV_K, conv_dim)), small((1, conv_dim)),
            small((1, heads)), small((heads, 1)), small((1, heads)), small((heads, 1)),
            small((1, width)), small((1, width)),
        ],
        out_specs=pl.BlockSpec((None, chunk, width), lambda bi, ci: (bi, ci, 0)),
        out_shape=jax.ShapeDtypeStruct((b, s, width), BF16),
        scratch_shapes=[
            pltpu.VMEM((2 * CONV_HALO, conv_dim), F32),
            pltpu.VMEM((SSD_STATE, width), F32),
            pltpu.VMEM((CONV_K - 1, chunk, chunk), BF16),
        ],
        compiler_params=_params(("parallel", "arbitrary")),
        name="conv_ssd",
    )(proj3, proj3, dtc3, dtr, conv_w, conv_b.reshape(1, conv_dim),
      dt_bias.reshape(1, heads), dt_bias.reshape(heads, 1), a_log.reshape(1, heads), a_log.reshape(heads, 1),
      jnp.repeat(d_skip, HEAD_DIM).reshape(1, width), ssd_norm_g.reshape(1, width))


def _outproj_router_kernel(attn_ref, y_ref, x_ref, ga_ref, wa_ref, wy_ref, gf_ref, rwt_ref,
                           x1_ref, h_ref, ri_ref, rg_ref, cum_ref, carry_ref, *, n_exp):
    a = _rms(attn_ref[...].astype(F32), ga_ref[...]).astype(BF16)
    x1 = x_ref[...] + _dot(a, wa_ref[...]) + _dot(y_ref[...], wy_ref[...])
    x1_ref[...] = x1
    h = _rms(x1, gf_ref[...])
    h_ref[...] = h.astype(h_ref.dtype)
    _route(h, rwt_ref[...], ri_ref, rg_ref, cum_ref, carry_ref, n_exp)


def _outproj_router(attn2, y2, x2, attn_norm_g, w_out, ffn_norm_g, router_w):
    t, d = x2.shape
    wa = attn2.shape[1]
    wy = y2.shape[1]
    n_exp = router_w.shape[1]
    blk = min(TOKEN_BLOCK, t)
    nb = t // blk
    w_bf = w_out.astype(BF16)
    row = lambda width: pl.BlockSpec((blk, width), lambda i: (i, 0))
    const = lambda shape: pl.BlockSpec(shape, lambda i: (0, 0))
    return pl.pallas_call(
        functools.partial(_outproj_router_kernel, n_exp=n_exp),
        grid=(nb,),
        in_specs=[row(wa), row(wy), row(d), const((1, wa)), const((wa, d)), const((wy, d)), const((1, d)),
                  const((n_exp, d))],
        out_specs=[
            row(d), row(d),
            pl.BlockSpec((4, blk), lambda i: (0, i)),
            pl.BlockSpec((2, blk), lambda i: (0, i)),
            pl.BlockSpec((None, n_exp, LANES), lambda i: (i, 0, 0)),
        ],
        out_shape=[
            jax.ShapeDtypeStruct((t, d), F32), jax.ShapeDtypeStruct((t, d), BF16),
            jax.ShapeDtypeStruct((4, t), jnp.int32),
            jax.ShapeDtypeStruct((2, t), F32),
            jax.ShapeDtypeStruct((nb, n_exp, LANES), jnp.int32),
        ],
        scratch_shapes=[pltpu.VMEM((n_exp, LANES), F32)],
        compiler_params=_params(("arbitrary",)),
        name="outproj_router",
    )(attn2, y2, x2, attn_norm_g.reshape(1, wa), w_bf[:wa], w_bf[wa:], ffn_norm_g.reshape(1, d), router_w.T)


def _swiglu(x, wg_ref, wu_ref, wd_ref):
    fc = wg_ref.shape[-1]
    sub = _ff_chunk(fc, FF_SUB)
    part = None
    for c in range(0, fc, sub):
        act = (_silu(_dot(x, wg_ref[:, c:c + sub])) * _dot(x, wu_ref[:, c:c + sub])).astype(BF16)
        piece = _dot(act, wd_ref[c:c + sub, :])
        part = piece if part is None else part + piece
    return part


def _outproj_ffn_kernel(attn_ref, y_ref, x_ref, ga_ref, wa_ref, wy_ref, gf_ref, wg_ref, wu_ref, wd_ref, gfin_ref,
                        o_ref, *, final_norm):
    a = _rms(attn_ref[...].astype(F32), ga_ref[...]).astype(BF16)
    x1 = x_ref[...] + _dot(a, wa_ref[...]) + _dot(y_ref[...], wy_ref[...])
    h = _rms(x1, gf_ref[...]).astype(BF16)
    out = x1 + _swiglu(h, wg_ref, wu_ref, wd_ref)
    o_ref[...] = _rms(out, gfin_ref[...]) if final_norm else out


def _ff_chunk(d_ff, target):
    best = LANES
    for c in range(LANES, min(d_ff, target) + 1, LANES):
        if d_ff % c == 0:
            best = c
    return best


def _outproj_dense_ffn(attn2, y2, x2, attn_norm_g, w_out, ffn_norm_g, wg, wu, wd, final_g, final_norm):
    t, d = x2.shape
    wa = attn2.shape[1]
    wy = y2.shape[1]
    d_ff = wg.shape[1]
    tm = min(ROW_TILE, t)
    w_bf = w_out.astype(BF16)
    row = lambda width: pl.BlockSpec((tm, width), lambda i: (i, 0))
    const = lambda shape: pl.BlockSpec(shape, lambda i: (0, 0))
    once = lambda shape: pl.BlockSpec(shape, lambda i: (0, 0), pipeline_mode=pl.Buffered(1))
    return pl.pallas_call(
        functools.partial(_outproj_ffn_kernel, final_norm=final_norm),
        grid=(t // tm,),
        in_specs=[
            row(wa), row(wy), row(d), const((1, wa)), once((wa, d)), once((wy, d)), const((1, d)),
            once((d, d_ff)), once((d, d_ff)), once((d_ff, d)), const((1, d)),
        ],
        out_specs=row(d),
        out_shape=jax.ShapeDtypeStruct((t, d), F32),
        compiler_params=_params(("parallel",)),
        name="outproj_dense_ffn",
    )(attn2, y2, x2, attn_norm_g.reshape(1, wa), w_bf[:wa], w_bf[wa:], ffn_norm_g.reshape(1, d),
      wg.astype(BF16), wu.astype(BF16), wd.astype(BF16), final_g.reshape(1, d))


def _route(h, router_wt, ri_ref, rg_ref, cum_ref, carry_ref, n_exp):
    blk = h.shape[0]

    @pl.when(pl.program_id(0) == 0)
    def _():
        carry_ref[...] = jnp.zeros(carry_ref.shape, F32)

    w_hi, w_lo = _bf16_pieces(router_wt, 2)
    h_hi, h_lo = _bf16_pieces(h, 2)
    logits = _dot_nt(w_hi, h_hi) + _dot_nt(w_hi, h_lo) + _dot_nt(w_lo, h_hi)
    eidx = lax.broadcasted_iota(jnp.int32, (n_exp, blk), 0)
    m1 = jnp.max(logits, axis=0, keepdims=True)
    i1 = jnp.min(jnp.where(logits == m1, eidx, n_exp), axis=0, keepdims=True)
    rest = jnp.where(eidx == i1, -jnp.inf, logits)
    m2 = jnp.max(rest, axis=0, keepdims=True)
    i2 = jnp.min(jnp.where(rest == m2, eidx, n_exp), axis=0, keepdims=True)
    e21 = jnp.exp(m2 - m1)
    g1 = 1.0 / (1.0 + e21)
    g2 = e21 * g1

    sel1 = eidx == i1
    sel2 = eidx == i2
    onehot = jnp.where(sel1 | sel2, 1.0, 0.0)
    row = lax.broadcasted_iota(jnp.int32, (blk, blk), 0)
    col = lax.broadcasted_iota(jnp.int32, (blk, blk), 1)
    tri_u = jnp.where(row <= col, 1.0, 0.0).astype(BF16)
    incl = _dot(onehot.astype(BF16), tri_u)
    carry = carry_ref[:, 0:1]
    rank = carry + incl - onehot
    r1 = jnp.sum(jnp.where(sel1, rank, 0.0), axis=0, keepdims=True)
    r2 = jnp.sum(jnp.where(sel2, rank, 0.0), axis=0, keepdims=True)
    total = carry + jnp.sum(onehot, axis=1, keepdims=True)
    carry_ref[...] = jnp.broadcast_to(total, carry_ref.shape)
    cum_ref[...] = jnp.broadcast_to(total, cum_ref.shape).astype(jnp.int32)

    ri_ref[0:1, :] = i1
    ri_ref[1:2, :] = i2
    ri_ref[2:3, :] = r1.astype(jnp.int32)
    ri_ref[3:4, :] = r2.astype(jnp.int32)
    rg_ref[0:1, :] = g1
    rg_ref[1:2, :] = g2


def _gather_kernel(b0_ref, b1_ref, pos_ref, gate_ref, h_ref, xs_ref, gs_ref, acc_ref, gacc_ref, *, tile, blk):
    i = pl.program_id(0)
    slot = i * tile + lax.broadcasted_iota(jnp.int32, (tile, 1), 0)
    acc_ref[...] = jnp.zeros(acc_ref.shape, F32)
    gacc_ref[...] = jnp.zeros(gacc_ref.shape, F32)

    def body(b, carry):
        pos = pos_ref[b]
        gate = gate_ref[b]
        hit1 = pos[0:1, :] == slot
        hit2 = pos[1:2, :] == slot
        onehot = jnp.where(hit1 | hit2, 1.0, 0.0).astype(BF16)
        start = pl.multiple_of(b * blk, blk)
        acc_ref[...] += _dot(onehot, h_ref[pl.ds(start, blk), :])
        gsel = jnp.where(hit1, gate[0:1, :], 0.0) + jnp.where(hit2, gate[1:2, :], 0.0)
        gacc_ref[...] += jnp.sum(gsel, axis=1, keepdims=True)
        return carry

    lax.fori_loop(b0_ref[i], b1_ref[i], body, 0)
    xs_ref[...] = acc_ref[...].astype(xs_ref.dtype)
    gs_ref[...] = gacc_ref[...]


def _gather(b0, b1, pos3, gate3, h2, n_tiles, tile):
    t, d = h2.shape
    nb, _, blk = pos3.shape
    grid_spec = pltpu.PrefetchScalarGridSpec(
        num_scalar_prefetch=2,
        grid=(n_tiles,),
        in_specs=[
            pl.BlockSpec((nb, 2, blk), lambda i, *_: (0, 0, 0)),
            pl.BlockSpec((nb, 2, blk), lambda i, *_: (0, 0, 0)),
            pl.BlockSpec((t, d), lambda i, *_: (0, 0), pipeline_mode=pl.Buffered(1)),
        ],
        out_specs=[
            pl.BlockSpec((tile, d), lambda i, *_: (i, 0)),
            pl.BlockSpec((tile, 1), lambda i, *_: (i, 0)),
        ],
        scratch_shapes=[pltpu.VMEM((tile, d), F32), pltpu.VMEM((tile, 1), F32)],
    )
    return pl.pallas_call(
        functools.partial(_gather_kernel, tile=tile, blk=blk),
        grid_spec=grid_spec,
        out_shape=[
            jax.ShapeDtypeStruct((n_tiles * tile, d), BF16),
            jax.ShapeDtypeStruct((n_tiles * tile, 1), F32),
        ],
        compiler_params=_params(("parallel",)),
        name="moe_gather",
    )(b0, b1, pos3, gate3, h2)


def _expert_kernel(te_ref, nv_ref, xs_ref, gs_ref, wg_ref, wu_ref, wd_ref, y_ref, acc_ref):
    i = pl.program_id(0)
    j = pl.program_id(1)
    valid = i < nv_ref[0]

    @pl.when((i == 0) & (j == 0))
    def _():
        acc_ref[...] = jnp.zeros(acc_ref.shape, F32)

    @pl.when(valid)
    def _():
        total = jnp.where(j == 0, 0.0, acc_ref[...]) + _swiglu(xs_ref[...], wg_ref, wu_ref, wd_ref)
        acc_ref[...] = total
        y_ref[...] = (total * gs_ref[...]).astype(y_ref.dtype)

    @pl.when(jnp.logical_not(valid))
    def _():
        y_ref[...] = jnp.zeros(y_ref.shape, y_ref.dtype)


def _experts(tile_expert, n_valid, xs, gs, wg, wu, wd, tile):
    p, d = xs.shape
    n_tiles = p // tile
    d_ff = wg.shape[2]
    fc = _ff_chunk(d_ff, 1792)
    nj = d_ff // fc

    def chunk(i, j, nv):
        return jnp.where(i < nv[0], j, nj - 1)

    grid_spec = pltpu.PrefetchScalarGridSpec(
        num_scalar_prefetch=2,
        grid=(n_tiles, nj),
        in_specs=[
            pl.BlockSpec((tile, d), lambda i, j, te, nv: (i, 0)),
            pl.BlockSpec((tile, 1), lambda i, j, te, nv: (i, 0)),
            pl.BlockSpec((None, d, fc), lambda i, j, te, nv: (te[i], 0, chunk(i, j, nv))),
            pl.BlockSpec((None, d, fc), lambda i, j, te, nv: (te[i], 0, chunk(i, j, nv))),
            pl.BlockSpec((None, fc, d), lambda i, j, te, nv: (te[i], chunk(i, j, nv), 0)),
        ],
        out_specs=pl.BlockSpec((tile, d), lambda i, j, te, nv: (i, 0)),
        scratch_shapes=[pltpu.VMEM((tile, d), F32)],
    )
    return pl.pallas_call(
        _expert_kernel,
        grid_spec=grid_spec,
        out_shape=jax.ShapeDtypeStruct((p, d), BF16),
        compiler_params=_params(("parallel", "arbitrary")),
        name="moe_experts",
    )(tile_expert, n_valid, xs, gs, wg.astype(BF16), wu.astype(BF16), wd.astype(BF16))


def _combine_kernel(ib_ref, it0_ref, it1_ref, fl_ref, ys0_ref, ys1_ref, pos_ref, x_ref, gfin_ref, o_ref, acc_ref,
                    *, tile, final_norm):
    w = pl.program_id(0)
    flags = fl_ref[w]

    @pl.when((flags & 1) != 0)
    def _():
        acc_ref[...] = x_ref[...]

    @pl.when((flags & 4) != 0)
    def _():
        lane = lax.broadcasted_iota(jnp.int32, (1, tile), 1)
        pos = pos_ref[...]

        def onehot(tile_index):
            slot = tile_index * tile + lane
            return (pos[:, 0:1] == slot) | (pos[:, 1:2] == slot)

        hit = jnp.concatenate([onehot(it0_ref[w]), onehot(it1_ref[w]) & ((flags & 8) != 0)], axis=1)
        rows = jnp.concatenate([ys0_ref[...], ys1_ref[...]], axis=0)
        acc_ref[...] += _dot(jnp.where(hit, 1.0, 0.0).astype(BF16), rows)

    @pl.when((flags & 2) != 0)
    def _():
        out = acc_ref[...]
        o_ref[...] = _rms(out, gfin_ref[...]) if final_norm else out


def _combine(step_block, step_tile0, step_tile1, step_flags, ys, pos_col, x1, final_g, tile, final_norm):
    t, d = x1.shape
    blk = min(TOKEN_BLOCK, t)
    n_steps = step_block.shape[0]
    grid_spec = pltpu.PrefetchScalarGridSpec(
        num_scalar_prefetch=4,
        grid=(n_steps,),
        in_specs=[
            pl.BlockSpec((tile, d), lambda w, ib, it0, it1, fl: (it0[w], 0)),
            pl.BlockSpec((tile, d), lambda w, ib, it0, it1, fl: (it1[w], 0)),
            pl.BlockSpec((blk, 2), lambda w, ib, it0, it1, fl: (ib[w], 0)),
            pl.BlockSpec((blk, d), lambda w, ib, it0, it1, fl: (ib[w], 0)),
            pl.BlockSpec((1, d), lambda w, ib, it0, it1, fl: (0, 0)),
        ],
        out_specs=pl.BlockSpec((blk, d), lambda w, ib, it0, it1, fl: (ib[w], 0)),
        scratch_shapes=[pltpu.VMEM((blk, d), F32)],
    )
    return pl.pallas_call(
        functools.partial(_combine_kernel, tile=tile, final_norm=final_norm),
        grid_spec=grid_spec,
        out_shape=jax.ShapeDtypeStruct((t, d), F32),
        compiler_params=_params(("arbitrary",)),
        name="moe_combine",
    )(step_block, step_tile0, step_tile1, step_flags, ys, ys, pos_col, x1, final_g.reshape(1, d))


def _moe(x1, h2, ri, rg, cum_blocks, wg, wu, wd, final_g, final_norm):
    t, d = x1.shape
    n_exp = wg.shape[0]
    blk = min(TOKEN_BLOCK, t)
    nb = t // blk
    tile = min(EXPERT_TILE, t)
    n_tiles = TOP_K * t // tile + n_exp
    i32 = jnp.int32

    cum = jnp.concatenate([jnp.zeros((1, n_exp), i32), cum_blocks[:, :, 0]], axis=0)
    counts = cum[nb]
    tiles_per = (counts + tile - 1) // tile
    tile_end = jnp.cumsum(tiles_per)
    tile_start = tile_end - tiles_per
    seg_start = tile_start * tile
    n_valid = tile_end[n_exp - 1]
    tile_ids = jnp.arange(n_tiles, dtype=i32)
    tile_expert = jnp.minimum(jnp.sum(tile_ids[:, None] >= tile_end[None, :], axis=1), n_exp - 1).astype(i32)
    tile_expert = jnp.where(tile_ids < n_valid, tile_expert, tile_expert[jnp.maximum(n_valid - 1, 0)])

    seg_of = jnp.sum(jnp.where(ri[0:2, :, None] == jnp.arange(n_exp, dtype=i32), seg_start, 0), axis=-1)
    pos = seg_of + ri[2:4]
    pos3 = pos.reshape(2, nb, blk).transpose(1, 0, 2)
    gate3 = rg.reshape(2, nb, blk).transpose(1, 0, 2)

    r0 = (tile_ids - tile_start[tile_expert]) * tile
    cum_e = cum[:, tile_expert]
    b0 = jnp.sum(cum_e[1:] <= r0[None, :], axis=0)
    b1 = jnp.sum(cum_e[:-1] < (r0 + tile)[None, :], axis=0)
    live = tile_ids < n_valid
    b0 = jnp.where(live, b0, 0).astype(i32)
    b1 = jnp.where(live, b1, 0).astype(i32)

    xs, gs = _gather(b0, b1, pos3, gate3, h2, n_tiles, tile)
    ys = _experts(tile_expert, n_valid.reshape(1).astype(i32), xs, gs, wg, wu, wd, tile)

    lo = seg_start[None, :] + cum[:-1]
    hi = seg_start[None, :] + cum[1:]
    ctile = min(COMBINE_TILE, tile)
    first_tile = lo // ctile
    n_pair = jnp.where(hi > lo, (hi - 1) // ctile - first_tile + 1, 0).reshape(-1)
    pair_end = jnp.cumsum(n_pair)
    pair_start = pair_end - n_pair
    n_items = nb * n_exp + n_tiles * (tile // ctile)
    total = pair_end[-1]
    w_ids = jnp.arange(n_items, dtype=i32)
    w_clamped = jnp.minimum(w_ids, jnp.maximum(total - 1, 0))
    pair = jnp.minimum(jnp.sum(w_clamped[:, None] >= pair_end[None, :], axis=1), nb * n_exp - 1)
    item_tile = (first_tile.reshape(-1)[pair] + w_clamped - pair_start[pair]).astype(i32)

    per_block = n_pair.reshape(nb, n_exp).sum(axis=1)
    item_start = jnp.cumsum(per_block) - per_block
    steps_per = (per_block + 1) // 2
    step_end = jnp.cumsum(steps_per)
    n_steps = (n_items + nb + 1) // 2
    s_ids = jnp.arange(n_steps, dtype=i32)
    s_clamped = jnp.minimum(s_ids, jnp.maximum(step_end[-1] - 1, 0))
    step_block = jnp.minimum(jnp.sum(s_clamped[:, None] >= step_end[None, :], axis=1), nb - 1).astype(i32)
    local = s_clamped - (step_end - steps_per)[step_block]
    first_item = item_start[step_block] + 2 * local
    has_second = 2 * local + 1 < per_block[step_block]
    live = s_ids < step_end[-1]
    step_tile0 = item_tile[jnp.minimum(first_item, n_items - 1)]
    step_tile1 = jnp.where(has_second, item_tile[jnp.minimum(first_item + 1, n_items - 1)], step_tile0)
    step_flags = ((live & (local == 0)) * 1 + (live & (local == steps_per[step_block] - 1)) * 2
                  + live * 4 + (live & has_second) * 8).astype(i32)

    return _combine(step_block, step_tile0, step_tile1, step_flags, ys, pos.T, x1, final_g, ctile, final_norm)


def kernel(x, mix_norm_g, w_in, conv_w, conv_b, dt_bias, a_log, d_skip, attn_norm_g, ssd_norm_g, w_out, ffn_norm_g, dense_w_gate, dense_w_up, dense_w_down, router_w, moe_w_gate, moe_w_up, moe_w_down, final_norm_g):
    b, s, d = x.shape
    depth = w_in.shape[0]
    attn_width = attn_norm_g.shape[1]
    heads = dt_bias.shape[1]
    n_main = w_in.shape[2] - heads
    x2 = x.reshape(b * s, d)
    for layer in range(depth):
        proj, dtc, dtr = _inproj(x2, mix_norm_g[layer], w_in[layer, :, :n_main], w_in[layer, :, n_main:])
        proj3 = proj.reshape(b, s, n_main)
        attn = _attention(proj3, attn_width)
        y = _ssd(proj3, dtc.reshape(b, s, heads), dtr, conv_w[layer], conv_b[layer], dt_bias[layer],
                 a_log[layer], d_skip[layer], ssd_norm_g[layer], attn_width)
        attn2 = attn.reshape(b * s, attn_width)
        y2 = y.reshape(b * s, -1)
        final = layer == depth - 1
        i = layer // 2
        if layer % 2 == 0:
            x2 = _outproj_dense_ffn(attn2, y2, x2, attn_norm_g[layer], w_out[layer], ffn_norm_g[layer],
                                    dense_w_gate[i], dense_w_up[i], dense_w_down[i], final_norm_g, final)
        else:
            x1, h2, ri, rg, cum_blocks = _outproj_router(attn2, y2, x2, attn_norm_g[layer], w_out[layer],
                                                         ffn_norm_g[layer], router_w[i])
            x2 = _moe(x1, h2, ri, rg, cum_blocks, moe_w_gate[i], moe_w_up[i], moe_w_down[i],
                      final_norm_g, final)
    return x2.reshape(b, s, d)
```

```python
import functools

import jax
import jax.numpy as jnp
from jax import lax
from jax.experimental import pallas as pl
from jax.experimental.pallas import tpu as pltpu

F32 = jnp.float32
BF16 = jnp.bfloat16

HEAD_DIM = 64
HEADS_PER_VREG = 2
LANES = 128
SSD_GROUPS = 2
SSD_STATE = 128
CONV_K = 4
CONV_HALO = 8
TOP_K = 2
EPS = 1e-5
NEG_BIG = -1e30
LOG2_E = 1.4426950408889634
EXP2_CLAMP = 126.0

ROW_TILE = 512
FF_SUB = 256
ATTN_TILE = 256
SSD_CHUNK = 256
TOKEN_BLOCK = 512
EXPERT_TILE = 512
COMBINE_TILE = 256
VMEM_LIMIT = 56 * 1024 * 1024


def _params(semantics):
    return pltpu.CompilerParams(dimension_semantics=semantics, vmem_limit_bytes=VMEM_LIMIT)


def _rms(x, g):
    var = jnp.mean(x * x, axis=-1, keepdims=True)
    return x * lax.rsqrt(var + EPS) * g


def _silu(x):
    half = 0.5 * x
    return half + half * jnp.tanh(half)


def _softplus(x):
    return jnp.maximum(x, 0.0) + jnp.log1p(jnp.exp(-jnp.abs(x)))


def _dot(a, b):
    return jnp.dot(a, b, preferred_element_type=F32)


def _dot_nt(a, b):
    return lax.dot_general(a, b, (((1,), (1,)), ((), ())), preferred_element_type=F32)


def _bf16_pieces(x, n):
    pieces = []
    for _ in range(n):
        piece = x.astype(BF16)
        pieces.append(piece)
        x = x - piece.astype(F32)
    return pieces


def _inproj_kernel(x_ref, g_ref, w_ref, wdt_ref, proj_ref, dtc_ref, dtr_ref):
    h = _rms(x_ref[...], g_ref[...]).astype(BF16)
    proj_ref[...] = _dot(h, w_ref[...]).astype(proj_ref.dtype)
    dt_raw = _dot(h, wdt_ref[...])
    n_dt = dtc_ref.shape[-1]
    dtc_ref[...] = dt_raw[:, :n_dt]
    dtr_ref[...] = dt_raw.T[:n_dt, :]


def _inproj(x2, g, w_main, w_dt):
    t, d = x2.shape
    n = w_main.shape[1]
    nh = w_dt.shape[1]
    w_dt_pad = jnp.pad(w_dt, ((0, 0), (0, LANES - nh))).astype(BF16)
    tm = min(ROW_TILE, t)
    return pl.pallas_call(
        _inproj_kernel,
        grid=(t // tm,),
        in_specs=[
            pl.BlockSpec((tm, d), lambda i: (i, 0)),
            pl.BlockSpec((1, d), lambda i: (0, 0)),
            pl.BlockSpec((d, n), lambda i: (0, 0)),
            pl.BlockSpec((d, LANES), lambda i: (0, 0)),
        ],
        out_specs=[
            pl.BlockSpec((tm, n), lambda i: (i, 0)),
            pl.BlockSpec((tm, nh), lambda i: (i, 0)),
            pl.BlockSpec((nh, tm), lambda i: (0, i)),
        ],
        out_shape=[
            jax.ShapeDtypeStruct((t, n), BF16),
            jax.ShapeDtypeStruct((t, nh), F32),
            jax.ShapeDtypeStruct((nh, t), F32),
        ],
        compiler_params=_params(("parallel",)),
        name="inproj",
    )(x2, g.reshape(1, d), w_main.astype(BF16), w_dt_pad)


def _attn_kernel(q_ref, k_ref, v_ref, o_ref, za_ref, zb_ref, wa_ref, wb_ref, bias_ref, *, tile, nq):
    i = pl.program_id(2)
    n_first = i + 1
    lane = lax.broadcasted_iota(jnp.int32, (tile, LANES), 1)
    row = lax.broadcasted_iota(jnp.int32, (tile, tile), 0)
    col = lax.broadcasted_iota(jnp.int32, (tile, tile), 1)
    m_incl = jnp.where(row >= col, 1.0, 0.0).astype(BF16)
    bias_ref[0] = jnp.zeros((tile, tile), F32)
    bias_ref[1] = jnp.where(col < row, 0.0, NEG_BIG)
    head_lanes = [(lane >= hh * HEAD_DIM) & (lane < (hh + 1) * HEAD_DIM) for hh in range(HEADS_PER_VREG)]

    def q_heads(t):
        q = q_ref[pl.ds(pl.multiple_of(t * tile, tile), tile), :].astype(F32) * (HEAD_DIM ** -0.5 * LOG2_E)
        return [jnp.where(m, q, 0.0).astype(BF16) for m in head_lanes]

    q_first, q_second = q_heads(i), q_heads(nq - 1 - i)

    def where_is(n):
        if n == 0:
            return True, i, i, True
        if n > nq // 2:
            t = nq - 1 - i
            return False, t, t - (n - n_first), False
        in_first = n < n_first
        t = jnp.where(in_first, i, nq - 1 - i)
        m = jnp.where(in_first, n, n - n_first)
        return in_first, t, t - m, m == 0

    def pick(cond, a, b):
        return (a if cond else b) if isinstance(cond, bool) else jnp.where(cond, a, b)

    def store_logits(n, z_ref):
        in_first, _, j, diag = where_is(n)
        ks = k_ref[pl.ds(pl.multiple_of(j * tile, tile), tile), :]
        for hh in range(HEADS_PER_VREG):
            z2 = _dot_nt(pick(in_first, q_first[hh], q_second[hh]), ks)
            if diag is not False:
                z2 = z2 + bias_ref[1 if diag is True else diag.astype(jnp.int32)]
            z_ref[hh] = z2

    def store_weights(n, z_ref, w_ref, carries):
        diag = where_is(n)[3]
        new_carries = []
        for hh, carry in enumerate(carries):
            z2 = z_ref[hh]
            sp2 = jnp.maximum(jnp.log2(1.0 + jnp.exp2(jnp.minimum(z2, EXP2_CLAMP))), z2)
            cs2 = _dot(sp2.astype(BF16), m_incl)
            if diag is not True:
                cs2 = cs2 + pick(diag, 0.0, carry)
            w = jnp.exp2(z_ref[hh] - cs2)
            w_ref[:, hh * tile:(hh + 1) * tile] = w.astype(BF16)
            new_carries.append(cs2[:, 0:1])
        return tuple(new_carries)

    def add_values(n, w_ref, acc):
        _, t, j, diag = where_is(n)
        vs = v_ref[pl.ds(pl.multiple_of(j * tile, tile), tile), :]
        v_heads = [jnp.where(m, vs, jnp.zeros_like(vs)) for m in head_lanes]
        pv = _dot(w_ref[...], jnp.concatenate(v_heads, axis=0))
        acc = pv if diag is True else pick(diag, 0.0, acc) + pv
        o_ref[pl.ds(pl.multiple_of(t * tile, tile), tile), :] = acc.astype(o_ref.dtype)
        return acc

    store_logits(0, za_ref)
    carries = (jnp.zeros((tile, 1), F32),) * HEADS_PER_VREG
    acc = jnp.zeros((tile, LANES), F32)
    for n in range(0, nq, 2):
        store_logits(n + 1, zb_ref)
        if n > 0:
            acc = add_values(n - 1, wb_ref, acc)
        carries = store_weights(n, za_ref, wa_ref, carries)
        store_logits(n + 2, za_ref)
        acc = add_values(n, wa_ref, acc)
        carries = store_weights(n + 1, zb_ref, wb_ref, carries)
    acc = add_values(nq - 1, wb_ref, acc)
    store_weights(nq, za_ref, wa_ref, carries)
    add_values(nq, wa_ref, acc)


def _attention(proj3, attn_width):
    b, s, _ = proj3.shape
    tile = min(ATTN_TILE, s // 2)
    nq = s // tile
    assert s % tile == 0 and nq % 2 == 0, "q tiles are processed in pairs (i, nq-1-i)"
    n_pairs = attn_width // LANES

    def whole_sequence(col):
        return pl.BlockSpec((None, s, LANES), lambda bi, p, i: (bi, 0, col * n_pairs + p))

    return pl.pallas_call(
        functools.partial(_attn_kernel, tile=tile, nq=nq),
        grid=(b, n_pairs, nq // 2),
        in_specs=[whole_sequence(0), whole_sequence(1), whole_sequence(2)],
        out_specs=whole_sequence(0),
        out_shape=jax.ShapeDtypeStruct((b, s, attn_width), BF16),
        scratch_shapes=[
            pltpu.VMEM((HEADS_PER_VREG, tile, tile), F32), pltpu.VMEM((HEADS_PER_VREG, tile, tile), F32),
            pltpu.VMEM((tile, HEADS_PER_VREG * tile), BF16), pltpu.VMEM((tile, HEADS_PER_VREG * tile), BF16),
            pltpu.VMEM((2, tile, tile), F32),
        ],
        compiler_params=_params(("parallel", "parallel", "arbitrary")),
        name="sb_attention",
    )(proj3, proj3, proj3)


def _ssd_kernel(xbc_ref, z_ref, dtc_ref, dtr_ref, cw_ref, cb_ref, dtb_c_ref, dtb_r_ref, alog_c_ref,
                alog_r_ref, dskip_ref, gn_ref, y_ref, ext_ref, state_ref, shift_ref, *, chunk, width, heads):
    c = pl.program_id(1)
    heads_per_group = heads // SSD_GROUPS
    group_width = heads_per_group * HEAD_DIM
    gs = SSD_GROUPS * SSD_STATE

    @pl.when(c == 0)
    def _():
        ext_ref[0:CONV_HALO, :] = jnp.zeros((CONV_HALO, ext_ref.shape[1]), F32)
        state_ref[...] = jnp.zeros(state_ref.shape, F32)
        row = lax.broadcasted_iota(jnp.int32, (chunk, chunk), 0)
        col = lax.broadcasted_iota(jnp.int32, (chunk, chunk), 1)
        for d in range(1, CONV_K):
            shift_ref[d - 1] = jnp.where(row - col == d, 1.0, 0.0).astype(BF16)

    u = xbc_ref[...]
    u32 = u.astype(F32)
    conv = cb_ref[...] + cw_ref[CONV_K - 1:CONV_K, :] * u32
    for d in range(1, CONV_K):
        conv = conv + cw_ref[CONV_K - 1 - d:CONV_K - d, :] * _dot(shift_ref[d - 1], u)
    ext_ref[CONV_HALO:2 * CONV_HALO, :] = u32[0:CONV_HALO]
    head = cb_ref[...]
    for k in range(CONV_K):
        head = head + cw_ref[k:k + 1, :] * ext_ref[pl.ds(CONV_HALO - (CONV_K - 1) + k, CONV_HALO), :]
    ext_ref[0:CONV_HALO, :] = u32[chunk - CONV_HALO:chunk]
    xbc = _silu(jnp.concatenate([head, conv[CONV_HALO:]], axis=0))
    xs = xbc[:, :width]
    bm = xbc[:, width:width + gs]
    cm = xbc[:, width + gs:]

    dt_c = _softplus(dtc_ref[...] + dtb_c_ref[...])
    dt_r = _softplus(dtr_ref[...] + dtb_r_ref[...])
    da_c = dt_c * (-jnp.exp(alog_c_ref[...]))
    da_r = dt_r * (-jnp.exp(alog_r_ref[...]))
    row = lax.broadcasted_iota(jnp.int32, (chunk, chunk), 0)
    col = lax.broadcasted_iota(jnp.int32, (chunk, chunk), 1)
    lower = row >= col
    tri_l = jnp.where(lower, 1.0, 0.0).astype(BF16)
    tri_u = jnp.where(row <= col, 1.0, 0.0).astype(BF16)
    cum_c = sum(_dot(tri_l, piece) for piece in _bf16_pieces(da_c, 3))
    cum_r = sum(_dot(piece, tri_u) for piece in _bf16_pieces(da_r, 3))
    last_c = cum_c[chunk - 1:chunk, :]
    last_r = cum_r[:, chunk - 1:chunk]
    tail_r = jnp.exp(last_r - cum_r)

    lane = lax.broadcasted_iota(jnp.int32, (chunk, LANES), 1)

    def per_head_lanes(a):
        first = lane[:a.shape[0]] < HEAD_DIM
        return jnp.concatenate([jnp.where(first, a[:, p:p + 1], a[:, p + 1:p + 2])
                                for p in range(0, heads, HEADS_PER_VREG)], axis=1)

    dt_x = per_head_lanes(dt_c)
    head_x = per_head_lanes(jnp.exp(cum_c))
    last_x = per_head_lanes(jnp.exp(last_c))
    xdt = xs * dt_x

    y_parts = []
    new_state = []
    for g in range(SSD_GROUPS):
        b_g = bm[:, g * SSD_STATE:(g + 1) * SSD_STATE]
        c_g = cm[:, g * SSD_STATE:(g + 1) * SSD_STATE]
        cb = _dot_nt(c_g.astype(BF16), b_g.astype(BF16))
        bt_g = b_g.T
        st_g = state_ref[:, g * group_width:(g + 1) * group_width]
        y_off = _dot(c_g.astype(BF16), st_g.astype(BF16))
        for p in range(group_width // LANES):
            lo = g * group_width + p * LANES
            xdt_p = xdt[:, lo:lo + LANES]
            y_p = jnp.zeros((chunk, LANES), F32)
            s_p = jnp.zeros((SSD_STATE, LANES), F32)
            for hh in range(HEADS_PER_VREG):
                h = lo // HEAD_DIM + hh
                head_lanes = (lane >= hh * HEAD_DIM) & (lane < (hh + 1) * HEAD_DIM)
                x_h = jnp.where(head_lanes, xdt_p, 0.0).astype(BF16)
                seg = cum_c[:, h:h + 1] - cum_r[h:h + 1, :]
                decay = jnp.exp(jnp.where(lower, seg, NEG_BIG))
                y_p = y_p + _dot((cb * decay).astype(BF16), x_h)
                s_p = s_p + _dot((bt_g * tail_r[h:h + 1, :]).astype(BF16), x_h)
            y_parts.append(y_p + y_off[:, p * LANES:(p + 1) * LANES] * head_x[:, lo:lo + LANES])
            new_state.append(s_p)
    y = jnp.concatenate(y_parts, axis=1) + dskip_ref[...] * xs
    state_ref[...] = state_ref[...] * last_x + jnp.concatenate(new_state, axis=1)

    y = y * _silu(z_ref[...].astype(F32))
    normed = []
    for g in range(SSD_GROUPS):
        yg = y[:, g * group_width:(g + 1) * group_width]
        normed.append(yg * lax.rsqrt(jnp.mean(yg * yg, axis=-1, keepdims=True) + EPS))
    y_ref[...] = (jnp.concatenate(normed, axis=1) * gn_ref[...]).astype(y_ref.dtype)


def _ssd(proj3, dtc3, dtr, conv_w, conv_b, dt_bias, a_log, d_skip, ssd_norm_g, attn_width):
    b, s, _ = proj3.shape
    heads = dt_bias.shape[0]
    width = heads * HEAD_DIM
    conv_dim = conv_w.shape[1]
    chunk = min(SSD_CHUNK, s)
    nc = s // chunk
    z_blk = 3 * attn_width // width
    xbc_blk = (3 * attn_width + width) // conv_dim
    small = lambda shape: pl.BlockSpec(shape, lambda bi, ci: (0, 0))
    return pl.pallas_call(
        functools.partial(_ssd_kernel, chunk=chunk, width=width, heads=heads),
        grid=(b, nc),
        in_specs=[
            pl.BlockSpec((None, chunk, conv_dim), lambda bi, ci: (bi, ci, xbc_blk)),
            pl.BlockSpec((None, chunk, width), lambda bi, ci: (bi, ci, z_blk)),
            pl.BlockSpec((None, chunk, heads), lambda bi, ci: (bi, ci, 0)),
            pl.BlockSpec((heads, chunk), lambda bi, ci: (0, bi * nc + ci)),
            small((CONV_K, conv_dim)), small((1, conv_dim)),
            small((1, heads)), small((heads, 1)), small((1, heads)), small((heads, 1)),
            small((1, width)), small((1, width)),
        ],
        out_specs=pl.BlockSpec((None, chunk, width), lambda bi, ci: (bi, ci, 0)),
        out_shape=jax.ShapeDtypeStruct((b, s, width), BF16),
        scratch_shapes=[
            pltpu.VMEM((2 * CONV_HALO, conv_dim), F32),
            pltpu.VMEM((SSD_STATE, width), F32),
            pltpu.VMEM((CONV_K - 1, chunk, chunk), BF16),
        ],
        compiler_params=_params(("parallel", "arbitrary")),
        name="conv_ssd",
    )(proj3, proj3, dtc3, dtr, conv_w, conv_b.reshape(1, conv_dim),
      dt_bias.reshape(1, heads), dt_bias.reshape(heads, 1), a_log.reshape(1, heads), a_log.reshape(heads, 1),
      jnp.repeat(d_skip, HEAD_DIM).reshape(1, width), ssd_norm_g.reshape(1, width))


def _outproj_router_kernel(attn_ref, y_ref, x_ref, ga_ref, wa_ref, wy_ref, gf_ref, rwt_ref,
                           x1_ref, h_ref, ri_ref, rg_ref, cum_ref, carry_ref, *, n_exp):
    a = _rms(attn_ref[...].astype(F32), ga_ref[...]).astype(BF16)
    x1 = x_ref[...] + _dot(a, wa_ref[...]) + _dot(y_ref[...], wy_ref[...])
    x1_ref[...] = x1
    h = _rms(x1, gf_ref[...])
    h_ref[...] = h.astype(h_ref.dtype)
    _route(h, rwt_ref[...], ri_ref, rg_ref, cum_ref, carry_ref, n_exp)


def _outproj_router(attn2, y2, x2, attn_norm_g, w_out, ffn_norm_g, router_w):
    t, d = x2.shape
    wa = attn2.shape[1]
    wy = y2.shape[1]
    n_exp = router_w.shape[1]
    blk = min(TOKEN_BLOCK, t)
    nb = t // blk
    w_bf = w_out.astype(BF16)
    row = lambda width: pl.BlockSpec((blk, width), lambda i: (i, 0))
    const = lambda shape: pl.BlockSpec(shape, lambda i: (0, 0))
    return pl.pallas_call(
        functools.partial(_outproj_router_kernel, n_exp=n_exp),
        grid=(nb,),
        in_specs=[row(wa), row(wy), row(d), const((1, wa)), const((wa, d)), const((wy, d)), const((1, d)),
                  const((n_exp, d))],
        out_specs=[
            row(d), row(d),
            pl.BlockSpec((4, blk), lambda i: (0, i)),
            pl.BlockSpec((2, blk), lambda i: (0, i)),
            pl.BlockSpec((None, n_exp, LANES), lambda i: (i, 0, 0)),
        ],
        out_shape=[
            jax.ShapeDtypeStruct((t, d), F32), jax.ShapeDtypeStruct((t, d), BF16),
            jax.ShapeDtypeStruct((4, t), jnp.int32),
            jax.ShapeDtypeStruct((2, t), F32),
            jax.ShapeDtypeStruct((nb, n_exp, LANES), jnp.int32),
        ],
        scratch_shapes=[pltpu.VMEM((n_exp, LANES), F32)],
        compiler_params=_params(("arbitrary",)),
        name="outproj_router",
    )(attn2, y2, x2, attn_norm_g.reshape(1, wa), w_bf[:wa], w_bf[wa:], ffn_norm_g.reshape(1, d), router_w.T)


def _swiglu(x, wg_ref, wu_ref, wd_ref):
    fc = wg_ref.shape[-1]
    sub = _ff_chunk(fc, FF_SUB)
    part = None
    for c in range(0, fc, sub):
        act = (_silu(_dot(x, wg_ref[:, c:c + sub])) * _dot(x, wu_ref[:, c:c + sub])).astype(BF16)
        piece = _dot(act, wd_ref[c:c + sub, :])
        part = piece if part is None else part + piece
    return part


def _outproj_ffn_kernel(attn_ref, y_ref, x_ref, ga_ref, wa_ref, wy_ref, gf_ref, wg_ref, wu_ref, wd_ref, gfin_ref,
                        o_ref, *, final_norm):
    a = _rms(attn_ref[...].astype(F32), ga_ref[...]).astype(BF16)
    x1 = x_ref[...] + _dot(a, wa_ref[...]) + _dot(y_ref[...], wy_ref[...])
    h = _rms(x1, gf_ref[...]).astype(BF16)
    out = x1 + _swiglu(h, wg_ref, wu_ref, wd_ref)
    o_ref[...] = _rms(out, gfin_ref[...]) if final_norm else out


def _ff_chunk(d_ff, target):
    best = LANES
    for c in range(LANES, min(d_ff, target) + 1, LANES):
        if d_ff % c == 0:
            best = c
    return best


def _outproj_dense_ffn(attn2, y2, x2, attn_norm_g, w_out, ffn_norm_g, wg, wu, wd, final_g, final_norm):
    t, d = x2.shape
    wa = attn2.shape[1]
    wy = y2.shape[1]
    d_ff = wg.shape[1]
    tm = min(ROW_TILE, t)
    w_bf = w_out.astype(BF16)
    row = lambda width: pl.BlockSpec((tm, width), lambda i: (i, 0))
    const = lambda shape: pl.BlockSpec(shape, lambda i: (0, 0))
    once = lambda shape: pl.BlockSpec(shape, lambda i: (0, 0), pipeline_mode=pl.Buffered(1))
    return pl.pallas_call(
        functools.partial(_outproj_ffn_kernel, final_norm=final_norm),
        grid=(t // tm,),
        in_specs=[
            row(wa), row(wy), row(d), const((1, wa)), once((wa, d)), once((wy, d)), const((1, d)),
            once((d, d_ff)), once((d, d_ff)), once((d_ff, d)), const((1, d)),
        ],
        out_specs=row(d),
        out_shape=jax.ShapeDtypeStruct((t, d), F32),
        compiler_params=_params(("parallel",)),
        name="outproj_dense_ffn",
    )(attn2, y2, x2, attn_norm_g.reshape(1, wa), w_bf[:wa], w_bf[wa:], ffn_norm_g.reshape(1, d),
      wg.astype(BF16), wu.astype(BF16), wd.astype(BF16), final_g.reshape(1, d))


def _route(h, router_wt, ri_ref, rg_ref, cum_ref, carry_ref, n_exp):
    blk = h.shape[0]

    @pl.when(pl.program_id(0) == 0)
    def _():
        carry_ref[...] = jnp.zeros(carry_ref.shape, F32)

    w_hi, w_lo = _bf16_pieces(router_wt, 2)
    h_hi, h_lo = _bf16_pieces(h, 2)
    logits = _dot_nt(w_hi, h_hi) + _dot_nt(w_hi, h_lo) + _dot_nt(w_lo, h_hi)
    eidx = lax.broadcasted_iota(jnp.int32, (n_exp, blk), 0)
    m1 = jnp.max(logits, axis=0, keepdims=True)
    i1 = jnp.min(jnp.where(logits == m1, eidx, n_exp), axis=0, keepdims=True)
    rest = jnp.where(eidx == i1, -jnp.inf, logits)
    m2 = jnp.max(rest, axis=0, keepdims=True)
    i2 = jnp.min(jnp.where(rest == m2, eidx, n_exp), axis=0, keepdims=True)
    e21 = jnp.exp(m2 - m1)
    g1 = 1.0 / (1.0 + e21)
    g2 = e21 * g1

    sel1 = eidx == i1
    sel2 = eidx == i2
    onehot = jnp.where(sel1 | sel2, 1.0, 0.0)
    row = lax.broadcasted_iota(jnp.int32, (blk, blk), 0)
    col = lax.broadcasted_iota(jnp.int32, (blk, blk), 1)
    tri_u = jnp.where(row <= col, 1.0, 0.0).astype(BF16)
    incl = _dot(onehot.astype(BF16), tri_u)
    carry = carry_ref[:, 0:1]
    rank = carry + incl - onehot
    r1 = jnp.sum(jnp.where(sel1, rank, 0.0), axis=0, keepdims=True)
    r2 = jnp.sum(jnp.where(sel2, rank, 0.0), axis=0, keepdims=True)
    total = carry + jnp.sum(onehot, axis=1, keepdims=True)
    carry_ref[...] = jnp.broadcast_to(total, carry_ref.shape)
    cum_ref[...] = jnp.broadcast_to(total, cum_ref.shape).astype(jnp.int32)

    ri_ref[0:1, :] = i1
    ri_ref[1:2, :] = i2
    ri_ref[2:3, :] = r1.astype(jnp.int32)
    ri_ref[3:4, :] = r2.astype(jnp.int32)
    rg_ref[0:1, :] = g1
    rg_ref[1:2, :] = g2


def _gather_kernel(b0_ref, b1_ref, pos_ref, gate_ref, h_ref, xs_ref, gs_ref, acc_ref, gacc_ref, *, tile, blk):
    i = pl.program_id(0)
    slot = i * tile + lax.broadcasted_iota(jnp.int32, (tile, 1), 0)
    acc_ref[...] = jnp.zeros(acc_ref.shape, F32)
    gacc_ref[...] = jnp.zeros(gacc_ref.shape, F32)

    def body(b, carry):
        pos = pos_ref[b]
        gate = gate_ref[b]
        hit1 = pos[0:1, :] == slot
        hit2 = pos[1:2, :] == slot
        onehot = jnp.where(hit1 | hit2, 1.0, 0.0).astype(BF16)
        start = pl.multiple_of(b * blk, blk)
        acc_ref[...] += _dot(onehot, h_ref[pl.ds(start, blk), :])
        gsel = jnp.where(hit1, gate[0:1, :], 0.0) + jnp.where(hit2, gate[1:2, :], 0.0)
        gacc_ref[...] += jnp.sum(gsel, axis=1, keepdims=True)
        return carry

    lax.fori_loop(b0_ref[i], b1_ref[i], body, 0)
    xs_ref[...] = acc_ref[...].astype(xs_ref.dtype)
    gs_ref[...] = gacc_ref[...]


def _gather(b0, b1, pos3, gate3, h2, n_tiles, tile):
    t, d = h2.shape
    nb, _, blk = pos3.shape
    grid_spec = pltpu.PrefetchScalarGridSpec(
        num_scalar_prefetch=2,
        grid=(n_tiles,),
        in_specs=[
            pl.BlockSpec((nb, 2, blk), lambda i, *_: (0, 0, 0)),
            pl.BlockSpec((nb, 2, blk), lambda i, *_: (0, 0, 0)),
            pl.BlockSpec((t, d), lambda i, *_: (0, 0), pipeline_mode=pl.Buffered(1)),
        ],
        out_specs=[
            pl.BlockSpec((tile, d), lambda i, *_: (i, 0)),
            pl.BlockSpec((tile, 1), lambda i, *_: (i, 0)),
        ],
        scratch_shapes=[pltpu.VMEM((tile, d), F32), pltpu.VMEM((tile, 1), F32)],
    )
    return pl.pallas_call(
        functools.partial(_gather_kernel, tile=tile, blk=blk),
        grid_spec=grid_spec,
        out_shape=[
            jax.ShapeDtypeStruct((n_tiles * tile, d), BF16),
            jax.ShapeDtypeStruct((n_tiles * tile, 1), F32),
        ],
        compiler_params=_params(("parallel",)),
        name="moe_gather",
    )(b0, b1, pos3, gate3, h2)


def _expert_kernel(te_ref, nv_ref, xs_ref, gs_ref, wg_ref, wu_ref, wd_ref, y_ref, acc_ref):
    i = pl.program_id(0)
    j = pl.program_id(1)
    valid = i < nv_ref[0]

    @pl.when((i == 0) & (j == 0))
    def _():
        acc_ref[...] = jnp.zeros(acc_ref.shape, F32)

    @pl.when(valid)
    def _():
        total = jnp.where(j == 0, 0.0, acc_ref[...]) + _swiglu(xs_ref[...], wg_ref, wu_ref, wd_ref)
        acc_ref[...] = total
        y_ref[...] = (total * gs_ref[...]).astype(y_ref.dtype)

    @pl.when(jnp.logical_not(valid))
    def _():
        y_ref[...] = jnp.zeros(y_ref.shape, y_ref.dtype)


def _experts(tile_expert, n_valid, xs, gs, wg, wu, wd, tile):
    p, d = xs.shape
    n_tiles = p // tile
    d_ff = wg.shape[2]
    fc = _ff_chunk(d_ff, 1792)
    nj = d_ff // fc

    def chunk(i, j, nv):
        return jnp.where(i < nv[0], j, nj - 1)

    grid_spec = pltpu.PrefetchScalarGridSpec(
        num_scalar_prefetch=2,
        grid=(n_tiles, nj),
        in_specs=[
            pl.BlockSpec((tile, d), lambda i, j, te, nv: (i, 0)),
            pl.BlockSpec((tile, 1), lambda i, j, te, nv: (i, 0)),
            pl.BlockSpec((None, d, fc), lambda i, j, te, nv: (te[i], 0, chunk(i, j, nv))),
            pl.BlockSpec((None, d, fc), lambda i, j, te, nv: (te[i], 0, chunk(i, j, nv))),
            pl.BlockSpec((None, fc, d), lambda i, j, te, nv: (te[i], chunk(i, j, nv), 0)),
        ],
        out_specs=pl.BlockSpec((tile, d), lambda i, j, te, nv: (i, 0)),
        scratch_shapes=[pltpu.VMEM((tile, d), F32)],
    )
    return pl.pallas_call(
        _expert_kernel,
        grid_spec=grid_spec,
        out_shape=jax.ShapeDtypeStruct((p, d), BF16),
        compiler_params=_params(("parallel", "arbitrary")),
        name="moe_experts",
    )(tile_expert, n_valid, xs, gs, wg.astype(BF16), wu.astype(BF16), wd.astype(BF16))


def _combine_kernel(ib_ref, it0_ref, it1_ref, fl_ref, ys0_ref, ys1_ref, pos_ref, x_ref, gfin_ref, o_ref, acc_ref,
                    *, tile, final_norm):
    w = pl.program_id(0)
    flags = fl_ref[w]

    @pl.when((flags & 1) != 0)
    def _():
        acc_ref[...] = x_ref[...]

    @pl.when((flags & 4) != 0)
    def _():
        lane = lax.broadcasted_iota(jnp.int32, (1, tile), 1)
        pos = pos_ref[...]

        def onehot(tile_index):
            slot = tile_index * tile + lane
            return (pos[:, 0:1] == slot) | (pos[:, 1:2] == slot)

        hit = jnp.concatenate([onehot(it0_ref[w]), onehot(it1_ref[w]) & ((flags & 8) != 0)], axis=1)
        rows = jnp.concatenate([ys0_ref[...], ys1_ref[...]], axis=0)
        acc_ref[...] += _dot(jnp.where(hit, 1.0, 0.0).astype(BF16), rows)

    @pl.when((flags & 2) != 0)
    def _():
        out = acc_ref[...]
        o_ref[...] = _rms(out, gfin_ref[...]) if final_norm else out


def _combine(step_block, step_tile0, step_tile1, step_flags, ys, pos_col, x1, final_g, tile, final_norm):
    t, d = x1.shape
    blk = min(TOKEN_BLOCK, t)
    n_steps = step_block.shape[0]
    grid_spec = pltpu.PrefetchScalarGridSpec(
        num_scalar_prefetch=4,
        grid=(n_steps,),
        in_specs=[
            pl.BlockSpec((tile, d), lambda w, ib, it0, it1, fl: (it0[w], 0)),
            pl.BlockSpec((tile, d), lambda w, ib, it0, it1, fl: (it1[w], 0)),
            pl.BlockSpec((blk, 2), lambda w, ib, it0, it1, fl: (ib[w], 0)),
            pl.BlockSpec((blk, d), lambda w, ib, it0, it1, fl: (ib[w], 0)),
            pl.BlockSpec((1, d), lambda w, ib, it0, it1, fl: (0, 0)),
        ],
        out_specs=pl.BlockSpec((blk, d), lambda w, ib, it0, it1, fl: (ib[w], 0)),
        scratch_shapes=[pltpu.VMEM((blk, d), F32)],
    )
    return pl.pallas_call(
        functools.partial(_combine_kernel, tile=tile, final_norm=final_norm),
        grid_spec=grid_spec,
        out_shape=jax.ShapeDtypeStruct((t, d), F32),
        compiler_params=_params(("arbitrary",)),
        name="moe_combine",
    )(step_block, step_tile0, step_tile1, step_flags, ys, ys, pos_col, x1, final_g.reshape(1, d))


def _moe(x1, h2, ri, rg, cum_blocks, wg, wu, wd, final_g, final_norm):
    t, d = x1.shape
    n_exp = wg.shape[0]
    blk = min(TOKEN_BLOCK, t)
    nb = t // blk
    tile = min(EXPERT_TILE, t)
    n_tiles = TOP_K * t // tile + n_exp
    i32 = jnp.int32

    cum = jnp.concatenate([jnp.zeros((1, n_exp), i32), cum_blocks[:, :, 0]], axis=0)
    counts = cum[nb]
    tiles_per = (counts + tile - 1) // tile
    tile_end = jnp.cumsum(tiles_per)
    tile_start = tile_end - tiles_per
    seg_start = tile_start * tile
    n_valid = tile_end[n_exp - 1]
    tile_ids = jnp.arange(n_tiles, dtype=i32)
    tile_expert = jnp.minimum(jnp.sum(tile_ids[:, None] >= tile_end[None, :], axis=1), n_exp - 1).astype(i32)
    tile_expert = jnp.where(tile_ids < n_valid, tile_expert, tile_expert[jnp.maximum(n_valid - 1, 0)])

    seg_of = jnp.sum(jnp.where(ri[0:2, :, None] == jnp.arange(n_exp, dtype=i32), seg_start, 0), axis=-1)
    pos = seg_of + ri[2:4]
    pos3 = pos.reshape(2, nb, blk).transpose(1, 0, 2)
    gate3 = rg.reshape(2, nb, blk).transpose(1, 0, 2)

    r0 = (tile_ids - tile_start[tile_expert]) * tile
    cum_e = cum[:, tile_expert]
    b0 = jnp.sum(cum_e[1:] <= r0[None, :], axis=0)
    b1 = jnp.sum(cum_e[:-1] < (r0 + tile)[None, :], axis=0)
    live = tile_ids < n_valid
    b0 = jnp.where(live, b0, 0).astype(i32)
    b1 = jnp.where(live, b1, 0).astype(i32)

    xs, gs = _gather(b0, b1, pos3, gate3, h2, n_tiles, tile)
    ys = _experts(tile_expert, n_valid.reshape(1).astype(i32), xs, gs, wg, wu, wd, tile)

    lo = seg_start[None, :] + cum[:-1]
    hi = seg_start[None, :] + cum[1:]
    ctile = min(COMBINE_TILE, tile)
    first_tile = lo // ctile
    n_pair = jnp.where(hi > lo, (hi - 1) // ctile - first_tile + 1, 0).reshape(-1)
    pair_end = jnp.cumsum(n_pair)
    pair_start = pair_end - n_pair
    n_items = nb * n_exp + n_tiles * (tile // ctile)
    total = pair_end[-1]
    w_ids = jnp.arange(n_items, dtype=i32)
    w_clamped = jnp.minimum(w_ids, jnp.maximum(total - 1, 0))
    pair = jnp.minimum(jnp.sum(w_clamped[:, None] >= pair_end[None, :], axis=1), nb * n_exp - 1)
    item_tile = (first_tile.reshape(-1)[pair] + w_clamped - pair_start[pair]).astype(i32)
    item_tile, n_pair = lax.optimization_barrier((item_tile, n_pair))

    per_block = n_pair.reshape(nb, n_exp).sum(axis=1)
    item_start = jnp.cumsum(per_block) - per_block
    steps_per = (per_block + 1) // 2
    step_end = jnp.cumsum(steps_per)
    n_steps = (n_items + nb + 1) // 2
    s_ids = jnp.arange(n_steps, dtype=i32)
    s_clamped = jnp.minimum(s_ids, jnp.maximum(step_end[-1] - 1, 0))
    step_block = jnp.minimum(jnp.sum(s_clamped[:, None] >= step_end[None, :], axis=1), nb - 1).astype(i32)
    local = s_clamped - (step_end - steps_per)[step_block]
    first_item = item_start[step_block] + 2 * local
    has_second = 2 * local + 1 < per_block[step_block]
    live = s_ids < step_end[-1]
    step_tile0 = item_tile[jnp.minimum(first_item, n_items - 1)]
    step_tile1 = jnp.where(has_second, item_tile[jnp.minimum(first_item + 1, n_items - 1)], step_tile0)
    step_flags = ((live & (local == 0)) * 1 + (live & (local == steps_per[step_block] - 1)) * 2
                  + live * 4 + (live & has_second) * 8).astype(i32)

    return _combine(step_block, step_tile0, step_tile1, step_flags, ys, pos.T, x1, final_g, ctile, final_norm)


def kernel(x, mix_norm_g, w_in, conv_w, conv_b, dt_bias, a_log, d_skip, attn_norm_g, ssd_norm_g, w_out, ffn_norm_g, dense_w_gate, dense_w_up, dense_w_down, router_w, moe_w_gate, moe_w_up, moe_w_down, final_norm_g):
    b, s, d = x.shape
    depth = w_in.shape[0]
    attn_width = attn_norm_g.shape[1]
    heads = dt_bias.shape[1]
    n_main = w_in.shape[2] - heads
    x2 = x.reshape(b * s, d)
    for layer in range(depth):
        proj, dtc, dtr = _inproj(x2, mix_norm_g[layer], w_in[layer, :, :n_main], w_in[layer, :, n_main:])
        proj3 = proj.reshape(b, s, n_main)
        attn = _attention(proj3, attn_width)
        y = _ssd(proj3, dtc.reshape(b, s, heads), dtr, conv_w[layer], conv_b[layer], dt_bias[layer],
                 a_log[layer], d_skip[layer], ssd_norm_g[layer], attn_width)
        attn2 = attn.reshape(b * s, attn_width)
        y2 = y.reshape(b * s, -1)
        final = layer == depth - 1
        i = layer // 2
        if layer % 2 == 0:
            x2 = _outproj_dense_ffn(attn2, y2, x2, attn_norm_g[layer], w_out[layer], ffn_norm_g[layer],
                                    dense_w_gate[i], dense_w_up[i], dense_w_down[i], final_norm_g, final)
        else:
            x1, h2, ri, rg, cum_blocks = _outproj_router(attn2, y2, x2, attn_norm_g[layer], w_out[layer],
                                                         ffn_norm_g[layer], router_w[i])
            x2 = _moe(x1, h2, ri, rg, cum_blocks, moe_w_gate[i], moe_w_up[i], moe_w_down[i],
                      final_norm_g, final)
    return x2.reshape(b, s, d)
```
